```python
import jax
import jax.numpy as jnp
from jax import lax
import numpy as np

D_MODEL = 4096
BATCH = 4
SEQ = 2048
DEPTH = 4

GRID_W = 64
CTX_LEN = 256
FOURIER_GROUPS = 4
FOURIER_GROUP_W = D_MODEL // 16
FOURIER_W = FOURIER_GROUPS * FOURIER_GROUP_W
CONV_W = D_MODEL // 4
CONV_K = 31
MLA_HEADS = D_MODEL // 256
Q_LORA = D_MODEL // 4
KV_LORA = D_MODEL // 8
QK_NOPE = 128
QK_ROPE = 64
V_DIM = 128
MLA_W = MLA_HEADS * V_DIM
ROPE_THETA = 10000.0
ROPE_FREQS = QK_ROPE // 4
ATTN_SCALE = (QK_NOPE + QK_ROPE) ** -0.5
Q_BLOCK = 128
N_BRANCH = 3
GATE_RANK = 512
MOD_RANK = 256
N_MOD = 6
D_FF = 4 * D_MODEL
EPS = 1e-6
OFF_C = FOURIER_W
OFF_Q = OFF_C + 2 * CONV_W
OFF_KV = OFF_Q + Q_LORA
D_IN = OFF_KV + KV_LORA + QK_ROPE

kernel_name = "hybrid_fourier_conformer_mla_dit"


def rms_norm(x, g):
    x32 = x.astype(jnp.float32)
    y = x32 * lax.rsqrt(jnp.mean(jnp.square(x32), axis=-1, keepdims=True) + EPS)
    return (y * g.astype(jnp.float32)).astype(x.dtype)


def layer_norm(x, g, b):
    x32 = x.astype(jnp.float32)
    xc = x32 - jnp.mean(x32, axis=-1, keepdims=True)
    y = xc * lax.rsqrt(jnp.mean(jnp.square(xc), axis=-1, keepdims=True) + EPS)
    return (y * g.astype(jnp.float32) + b.astype(jnp.float32)).astype(x.dtype)


def axial_rope_tables(n_tokens):
    n_rows = n_tokens // GRID_W
    rows = jnp.repeat(jnp.arange(n_rows, dtype=jnp.float32), GRID_W)
    cols = jnp.tile(jnp.arange(GRID_W, dtype=jnp.float32), n_rows)
    pos = jnp.stack([rows, cols], axis=-1)
    inv_freq = jnp.power(ROPE_THETA, -jnp.arange(ROPE_FREQS, dtype=jnp.float32) / ROPE_FREQS)
    ang = pos[:, :, None] * inv_freq
    return jnp.cos(ang), jnp.sin(ang)


def apply_rope(x, cos, sin):
    xs = x.reshape(x.shape[:-1] + (2, 2, ROPE_FREQS))
    x1, x2 = xs[..., 0, :], xs[..., 1, :]
    cos = cos.astype(x.dtype)
    sin = sin.astype(x.dtype)
    out = jnp.stack([x1 * cos - x2 * sin, x2 * cos + x1 * sin], axis=-2)
    return out.reshape(x.shape)


def adaln(cond, w_a, w_b, b):
    m = (jax.nn.silu(cond) @ w_a) @ w_b + b
    return jnp.split(m, N_MOD, axis=-1)


def modulate(x, g, shift, scale):
    return rms_norm(x, g) * (1 + scale) + shift


def fourier_mix(u):
    b, l, _ = u.shape
    ug = u.astype(jnp.float32).reshape(b, l, FOURIER_GROUPS, FOURIER_GROUP_W)
    f = jnp.fft.fft2(ug, axes=(1, 3), norm="ortho").real
    return f.reshape(b, l, FOURIER_W).astype(u.dtype)


def conformer_conv(u, conv_w, conv_b, ln_g, ln_b):
    a, gt = jnp.split(u, 2, axis=-1)
    v = a * jax.nn.sigmoid(gt)
    y = lax.conv_general_dilated(
        v, conv_w[:, None, :].astype(v.dtype), window_strides=(1,),
        padding=((CONV_K // 2, CONV_K // 2),), dimension_numbers=("NWC", "WIO", "NWC"),
        feature_group_count=CONV_W)
    return jax.nn.silu(layer_norm(y + conv_b, ln_g, ln_b))


def mla_queries(cq, q_norm_g, w_uq, rope):
    b, l, _ = cq.shape
    q = (rms_norm(cq, q_norm_g) @ w_uq).reshape(b, l, MLA_HEADS, QK_NOPE + QK_ROPE)
    qn, qr = q[..., :QK_NOPE], q[..., QK_NOPE:]
    if rope is not None:
        qr = apply_rope(qr, rope[0][:, None], rope[1][:, None])
    return qn, qr


def mla_keys(ckv, kv_norm_g, w_ukv, rope):
    b, l, _ = ckv.shape
    c_kv, kr = ckv[..., :KV_LORA], ckv[..., KV_LORA:]
    kv = (rms_norm(c_kv, kv_norm_g) @ w_ukv).reshape(b, l, MLA_HEADS, QK_NOPE + V_DIM)
    kn, v = kv[..., :QK_NOPE], kv[..., QK_NOPE:]
    if rope is not None:
        kr = apply_rope(kr, rope[0], rope[1])
    return kn, kr, v


def mla_attend(qn, qr, kn, kr, v):
    s = (jnp.einsum("bqhd,bkhd->bhqk", qn, kn, preferred_element_type=jnp.float32)
         + jnp.einsum("bqhr,bkr->bhqk", qr, kr, preferred_element_type=jnp.float32))
    p = jax.nn.softmax(s * ATTN_SCALE, axis=-1).astype(v.dtype)
    return jnp.einsum("bhqk,bkhd->bqhd", p, v)


def mla_attend_blocked(qn, qr, kn, kr, v):
    b, l, h, _ = qn.shape
    nb = l // Q_BLOCK

    def to_blocks(t):
        return jnp.moveaxis(t.reshape((b, nb, Q_BLOCK) + t.shape[2:]), 1, 0)

    o = lax.map(lambda q: mla_attend(q[0], q[1], kn, kr, v), (to_blocks(qn), to_blocks(qr)))
    return jnp.moveaxis(o, 0, 1).reshape(b, l, h, V_DIM)


def token_mixer(h, u, own_kv, ctx_kv, rope, lp):
    b, l, _ = h.shape
    y_f = fourier_mix(u[..., :OFF_C]) @ lp["w_pf"]
    y_c = conformer_conv(u[..., OFF_C:OFF_Q], lp["conv_w"], lp["conv_b"],
                         lp["conv_ln_g"], lp["conv_ln_b"]) @ lp["w_pc"]
    qn, qr = mla_queries(u[..., OFF_Q:OFF_KV], lp["q_norm_g"], lp["w_uq"], rope)
    kn, kr, v = own_kv
    if ctx_kv is None:
        o = mla_attend(qn, qr, kn, kr, v)
    else:
        kn = jnp.concatenate([kn, ctx_kv[0]], axis=1)
        kr = jnp.concatenate([kr, ctx_kv[1]], axis=1)
        v = jnp.concatenate([v, ctx_kv[2]], axis=1)
        o = mla_attend_blocked(qn, qr, kn, kr, v)
    y_m = o.reshape(b, l, MLA_W) @ lp["w_pm"]
    gates = jax.nn.sigmoid((h @ lp["w_gate_a"]) @ lp["w_gate_b"] + lp["b_gate"])
    g_f, g_c, g_m = jnp.split(gates, N_BRANCH, axis=-1)
    return (g_f * y_f + g_c * y_c + g_m * y_m) @ lp["w_out"]


def sq_relu_mlp(h, w1, w2):
    return jnp.square(jax.nn.relu(h @ w1)) @ w2


def setup_inputs(seed: int = 0) -> dict:
    key = jax.random.key(seed)
    ks = list(jax.random.split(key, 32))

    def nrm(k, shape, scale):
        return jax.random.normal(k, shape, jnp.float32) * scale

    def gain(k, shape):
        return 1.0 + 0.05 * jax.random.normal(k, shape, jnp.float32)

    L = DEPTH
    return {
        "x": nrm(ks[0], (BATCH, SEQ, D_MODEL), 1.0),
        "c": nrm(ks[1], (BATCH, D_MODEL), 1.0),
        "ctx": nrm(ks[2], (BATCH, CTX_LEN, D_MODEL), 1.0),
        "c_ctx": nrm(ks[3], (D_MODEL,), 1.0),
        "g_mix_pre": gain(ks[4], (L, D_MODEL)),
        "g_mix_post": gain(ks[5], (L, D_MODEL)),
        "g_mlp_pre": gain(ks[6], (L, D_MODEL)),
        "g_mlp_post": gain(ks[7], (L, D_MODEL)),
        "w_mod_a": nrm(ks[8], (L, D_MODEL, MOD_RANK), D_MODEL ** -0.5),
        "w_mod_b": nrm(ks[9], (L, MOD_RANK, N_MOD * D_MODEL), 0.5 * MOD_RANK ** -0.5),
        "b_mod": nrm(ks[10], (L, N_MOD * D_MODEL), 0.02),
        "w_in": nrm(ks[11], (L, D_MODEL, D_IN), D_MODEL ** -0.5),
        "conv_w": nrm(ks[12], (L, CONV_K, CONV_W), CONV_K ** -0.5),
        "conv_b": nrm(ks[13], (L, CONV_W), 0.02),
        "conv_ln_g": gain(ks[14], (L, CONV_W)),
        "conv_ln_b": nrm(ks[15], (L, CONV_W), 0.02),
        "q_norm_g": gain(ks[16], (L, Q_LORA)),
        "w_uq": nrm(ks[17], (L, Q_LORA, MLA_HEADS * (QK_NOPE + QK_ROPE)), Q_LORA ** -0.5),
        "kv_norm_g": gain(ks[18], (L, KV_LORA)),
        "w_ukv": nrm(ks[19], (L, KV_LORA, MLA_HEADS * (QK_NOPE + V_DIM)), KV_LORA ** -0.5),
        "w_pf": nrm(ks[20], (L, FOURIER_W, D_MODEL), FOURIER_W ** -0.5),
        "w_pc": nrm(ks[21], (L, CONV_W, D_MODEL), CONV_W ** -0.5),
        "w_pm": nrm(ks[22], (L, MLA_W, D_MODEL), MLA_W ** -0.5),
        "w_gate_a": nrm(ks[23], (L, D_MODEL, GATE_RANK), D_MODEL ** -0.5),
        "w_gate_b": nrm(ks[24], (L, GATE_RANK, N_BRANCH * D_MODEL), GATE_RANK ** -0.5),
        "b_gate": nrm(ks[25], (L, N_BRANCH * D_MODEL), 0.02),
        "w_out": nrm(ks[26], (L, D_MODEL, D_MODEL), D_MODEL ** -0.5),
        "w_ff1": nrm(ks[27], (L, D_MODEL, D_FF), D_MODEL ** -0.5),
        "w_ff2": nrm(ks[28], (L, D_FF, D_MODEL), D_FF ** -0.5),
    }


def reference(x, c, ctx, c_ctx, g_mix_pre, g_mix_post, g_mlp_pre, g_mlp_post,
              w_mod_a, w_mod_b, b_mod, w_in, conv_w, conv_b, conv_ln_g, conv_ln_b,
              q_norm_g, w_uq, kv_norm_g, w_ukv, w_pf, w_pc, w_pm,
              w_gate_a, w_gate_b, b_gate, w_out, w_ff1, w_ff2):
    rope = axial_rope_tables(x.shape[1])
    t = ctx
    for i in range(DEPTH):
        last = i == DEPTH - 1
        lp = {
            "conv_w": conv_w[i], "conv_b": conv_b[i], "conv_ln_g": conv_ln_g[i],
            "conv_ln_b": conv_ln_b[i], "q_norm_g": q_norm_g[i], "w_uq": w_uq[i],
            "w_pf": w_pf[i], "w_pc": w_pc[i], "w_pm": w_pm[i],
            "w_gate_a": w_gate_a[i], "w_gate_b": w_gate_b[i], "b_gate": b_gate[i],
            "w_out": w_out[i],
        }
        sx_a, cx_a, gx_a, sx_m, cx_m, gx_m = adaln(c[:, None, :], w_mod_a[i], w_mod_b[i], b_mod[i])
        st_a, ct_a, gt_a, st_m, ct_m, gt_m = adaln(c_ctx[None, None, :], w_mod_a[i], w_mod_b[i], b_mod[i])

        ht = modulate(t, g_mix_pre[i], st_a, ct_a)
        ut = ht @ w_in[i]
        kv_t = mla_keys(ut[..., OFF_KV:], kv_norm_g[i], w_ukv[i], None)
        hx = modulate(x, g_mix_pre[i], sx_a, cx_a)
        ux = hx @ w_in[i]
        kv_x = mla_keys(ux[..., OFF_KV:], kv_norm_g[i], w_ukv[i], rope)
        yx = token_mixer(hx, ux, kv_x, kv_t, rope, lp)
        x = x + gx_a * rms_norm(yx, g_mix_post[i])

        hx = modulate(x, g_mlp_pre[i], sx_m, cx_m)
        x = x + gx_m * rms_norm(sq_relu_mlp(hx, w_ff1[i], w_ff2[i]), g_mlp_post[i])

        if not last:
            yt = token_mixer(ht, ut, kv_t, None, None, lp)
            t = t + gt_a * rms_norm(yt, g_mix_post[i])
            ht = modulate(t, g_mlp_pre[i], st_m, ct_m)
            t = t + gt_m * rms_norm(sq_relu_mlp(ht, w_ff1[i], w_ff2[i]), g_mlp_post[i])
    return x
```

```python
import functools

import jax
import jax.numpy as jnp
from jax import lax
from jax.experimental import pallas as pl
from jax.experimental.pallas import tpu as pltpu

F32 = jnp.float32
BF16 = jnp.bfloat16

D_MODEL = 4096
BATCH = 4
SEQ = 2048
DEPTH = 4
GRID_W = 64
CTX_LEN = 256
FOURIER_GROUPS = 4
FOURIER_GROUP_W = D_MODEL // 16
FOURIER_W = FOURIER_GROUPS * FOURIER_GROUP_W
CONV_W = D_MODEL // 4
CONV_K = 31
MLA_HEADS = D_MODEL // 256
Q_LORA = D_MODEL // 4
KV_LORA = D_MODEL // 8
QK_NOPE = 128
QK_ROPE = 64
V_DIM = 128
MLA_W = MLA_HEADS * V_DIM
ROPE_THETA = 10000.0
ROPE_FREQS = QK_ROPE // 4
ATTN_SCALE = (QK_NOPE + QK_ROPE) ** -0.5
N_BRANCH = 3
GATE_RANK = 512
MOD_RANK = 256
N_MOD = 6
D_FF = 4 * D_MODEL
EPS = 1e-6
OFF_C = FOURIER_W
OFF_Q = OFF_C + 2 * CONV_W
OFF_KV = OFF_Q + Q_LORA

ROWS_X = BATCH * SEQ
ROWS_C = BATCH * CTX_LEN
ROWS = ROWS_X + ROWS_C
N_GROUPS = 8
CTX_GROUP = BATCH
HEAD_W = 2 * QK_NOPE
CONV_PAD = 16

U_F = 0
U_A = U_F + FOURIER_W
U_G = U_A + CONV_W
U_Q = U_G + CONV_W
U_HG = U_Q + Q_LORA
U_CKV = U_HG + GATE_RANK
U_KR = U_CKV + KV_LORA
U_W = 5376

VMEM_LIMIT = 56 * 1024 * 1024


def _cparams(sem, limit=VMEM_LIMIT):
    return pltpu.CompilerParams(dimension_semantics=sem, vmem_limit_bytes=limit)


def _group_of_tile(i, tm):
    return jnp.where(i < ROWS_X // tm, i // (SEQ // tm), CTX_GROUP)


def _rope_block_of_tile(i, tm):
    return jnp.where(i < ROWS_X // tm, i % (SEQ // tm), SEQ // tm)


def _rms(x, g):
    return x * lax.rsqrt(jnp.mean(x * x, axis=-1, keepdims=True) + EPS) * g


def _adaln_kernel(c_ref, wa_ref, wb_ref, b_ref, o_ref):
    c = c_ref[...]
    s = (c * jax.nn.sigmoid(c)).astype(BF16)
    t = jnp.dot(s, wa_ref[...], preferred_element_type=F32)
    o_ref[...] = jnp.dot(t.astype(BF16), wb_ref[...], preferred_element_type=F32) + b_ref[...]


def _adaln(cond, w_a, w_b, b):
    tn = D_MODEL
    n = N_MOD * D_MODEL
    return pl.pallas_call(
        _adaln_kernel,
        grid=(DEPTH, n // tn),
        in_specs=[
            pl.BlockSpec((N_GROUPS, D_MODEL), lambda l, j: (0, 0)),
            pl.BlockSpec((None, D_MODEL, MOD_RANK), lambda l, j: (l, 0, 0)),
            pl.BlockSpec((None, MOD_RANK, tn), lambda l, j: (l, 0, j)),
            pl.BlockSpec((None, 1, tn), lambda l, j: (l, 0, j)),
        ],
        out_specs=pl.BlockSpec((None, N_GROUPS, tn), lambda l, j: (l, 0, j)),
        out_shape=jax.ShapeDtypeStruct((DEPTH, N_GROUPS, n), F32),
        compiler_params=_cparams(("parallel", "parallel")),
        name="adaln",
    )(cond, w_a, w_b, b)


def _modulate_kernel(x_ref, g_ref, sh_ref, sc_ref, o_ref):
    y = _rms(x_ref[...], g_ref[...])
    o_ref[...] = (y * (1.0 + sc_ref[...]) + sh_ref[...]).astype(o_ref.dtype)


def _mod_spec(k, tm):
    return pl.BlockSpec((None, None, 1, D_MODEL), lambda i: (k, _group_of_tile(i, tm), 0, 0))


def _modulate(x, g, mod, k_shift, k_scale):
    tm = 256
    return pl.pallas_call(
        _modulate_kernel,
        grid=(ROWS // tm,),
        in_specs=[
            pl.BlockSpec((tm, D_MODEL), lambda i: (i, 0)),
            pl.BlockSpec((1, D_MODEL), lambda i: (0, 0)),
            _mod_spec(k_shift, tm),
            _mod_spec(k_scale, tm),
        ],
        out_specs=pl.BlockSpec((tm, D_MODEL), lambda i: (i, 0)),
        out_shape=jax.ShapeDtypeStruct((ROWS, D_MODEL), BF16),
        compiler_params=_cparams(("parallel",)),
        name="modulate",
    )(x, g, mod, mod)


def _residual_kernel(x_ref, y_ref, g_ref, gate_ref, o_ref):
    o_ref[...] = x_ref[...] + gate_ref[...] * _rms(y_ref[...].astype(F32), g_ref[...])


def _residual(x, y, g, mod, k_gate):
    tm = 256
    return pl.pallas_call(
        _residual_kernel,
        grid=(ROWS // tm,),
        in_specs=[
            pl.BlockSpec((tm, D_MODEL), lambda i: (i, 0)),
            pl.BlockSpec((tm, D_MODEL), lambda i: (i, 0)),
            pl.BlockSpec((1, D_MODEL), lambda i: (0, 0)),
            _mod_spec(k_gate, tm),
        ],
        out_specs=pl.BlockSpec((tm, D_MODEL), lambda i: (i, 0)),
        out_shape=jax.ShapeDtypeStruct((ROWS, D_MODEL), F32),
        compiler_params=_cparams(("parallel",)),
        name="residual",
    )(x, y, g, mod)


def _mm_kernel(a_ref, w_ref, o_ref, *scratch, nk, sq_relu):
    def finish(r):
        if sq_relu:
            r = jnp.square(jnp.maximum(r, 0.0))
        o_ref[...] = r.astype(o_ref.dtype)

    p = jnp.dot(a_ref[...], w_ref[...], preferred_element_type=F32)
    if nk == 1:
        finish(p)
        return
    acc_ref, = scratch
    k = pl.program_id(2)

    @pl.when(k == 0)
    def _():
        acc_ref[...] = p

    @pl.when(k > 0)
    def _():
        acc_ref[...] += p

    @pl.when(k == nk - 1)
    def _():
        finish(acc_ref[...])


def _matmul(a, w, *, tm, tn, tk, out_dtype, sq_relu=False, name):
    m, kdim = a.shape
    n = w.shape[1]
    nk = kdim // tk
    return pl.pallas_call(
        functools.partial(_mm_kernel, nk=nk, sq_relu=sq_relu),
        grid=(m // tm, n // tn, nk),
        in_specs=[
            pl.BlockSpec((tm, tk), lambda i, j, k: (i, k)),
            pl.BlockSpec((tk, tn), lambda i, j, k: (k, j)),
        ],
        out_specs=pl.BlockSpec((tm, tn), lambda i, j, k: (i, j)),
        out_shape=jax.ShapeDtypeStruct((m, n), out_dtype),
        scratch_shapes=[pltpu.VMEM((tm, tn), F32)] if nk > 1 else [],
        compiler_params=_cparams(("parallel", "parallel", "arbitrary")),
        name=name,
    )(a, w)


def _chdft_kernel(u_ref, wc_ref, ws_ref, ab_ref):
    gw = FOURIER_GROUP_W
    for g in range(FOURIER_GROUPS):
        ug = u_ref[:, g * gw:(g + 1) * gw]
        ab_ref[:, g * gw:(g + 1) * gw] = jnp.dot(
            ug, wc_ref[...], preferred_element_type=F32).astype(BF16)
        ab_ref[:, FOURIER_W + g * gw:FOURIER_W + (g + 1) * gw] = jnp.dot(
            ug, ws_ref[...], preferred_element_type=F32).astype(BF16)


def _channel_dft(u, wc, ws):
    tm = 1024
    gw = FOURIER_GROUP_W
    return pl.pallas_call(
        _chdft_kernel,
        grid=(ROWS // tm,),
        in_specs=[
            pl.BlockSpec((tm, FOURIER_W), lambda i: (i, U_F // FOURIER_W)),
            pl.BlockSpec((gw, gw), lambda i: (0, 0)),
            pl.BlockSpec((gw, gw), lambda i: (0, 0)),
        ],
        out_specs=pl.BlockSpec((tm, 2 * FOURIER_W), lambda i: (i, 0)),
        out_shape=jax.ShapeDtypeStruct((ROWS, 2 * FOURIER_W), BF16),
        compiler_params=_cparams(("parallel",)),
        name="channel_dft",
    )(u, wc, ws)


def _seqdft_kernel(cl_ref, sl_ref, a_ref, b_ref, o_ref, *, scale):
    r = jnp.dot(cl_ref[...], a_ref[...], preferred_element_type=F32)
    r = r + jnp.dot(sl_ref[...], b_ref[...], preferred_element_type=F32)
    o_ref[...] = (r * scale).astype(o_ref.dtype)


def _seq_dft(ab, cl, sl, *, seq, row0):
    tm = min(seq, 1024)
    nm = seq // tm
    blk0 = row0 // seq
    return pl.pallas_call(
        functools.partial(_seqdft_kernel, scale=float(seq) ** -0.5),
        grid=(nm, BATCH),
        in_specs=[
            pl.BlockSpec((tm, seq), lambda m, b: (m, 0)),
            pl.BlockSpec((tm, seq), lambda m, b: (m, 0)),
            pl.BlockSpec((seq, FOURIER_W), lambda m, b: (blk0 + b, 0)),
            pl.BlockSpec((seq, FOURIER_W), lambda m, b: (blk0 + b, 1)),
        ],
        out_specs=pl.BlockSpec((tm, FOURIER_W), lambda m, b: (b * nm + m, 0)),
        out_shape=jax.ShapeDtypeStruct((BATCH * seq, FOURIER_W), BF16),
        compiler_params=_cparams(("parallel", "parallel")),
        name="seq_dft_%d" % seq,
    )(cl, sl, ab, ab)


def _dft_tables(n):
    idx = jnp.arange(n, dtype=jnp.int32)
    ang = ((idx[:, None] * idx[None, :]) % n).astype(F32) * (2.0 * jnp.pi / n)
    return jnp.cos(ang), jnp.sin(ang)


CONV_ROWS = 64
LANES = 128


def _conv_kernel(a_ref, gt_ref, w_ref, b_ref, lg_ref, lb_ref, o_ref, vpad_ref, y_ref, *, seq):
    zeros = jnp.zeros((CONV_PAD, CONV_W), F32)
    vpad_ref[0:CONV_PAD, :] = zeros
    vpad_ref[CONV_PAD + seq:2 * CONV_PAD + seq, :] = zeros
    glu_rows = 256

    def glu(i, carry):
        r0 = pl.multiple_of(i * glu_rows, glu_rows)
        a = a_ref[pl.ds(r0, glu_rows), :].astype(F32)
        g = gt_ref[pl.ds(r0, glu_rows), :].astype(F32)
        vpad_ref[pl.ds(r0 + CONV_PAD, glu_rows), :] = a * jax.nn.sigmoid(g)
        return carry

    lax.fori_loop(0, seq // glu_rows, glu, 0)

    win_rows = CONV_ROWS + 2 * CONV_PAD

    def step(i, carry):
        r0 = pl.multiple_of(i * CONV_ROWS, CONV_ROWS)
        for c in range(CONV_W // LANES):
            cols = slice(c * LANES, (c + 1) * LANES)
            win = vpad_ref[pl.ds(r0, win_rows), cols]
            acc = jnp.zeros((CONV_ROWS, LANES), F32)
            for b in range(8):
                wb = win if b == 0 else pltpu.roll(win, win_rows - b, axis=0)
                for a in range(4):
                    j = 8 * a + b
                    if j == 0:
                        continue
                    acc = acc + wb[8 * a:8 * a + CONV_ROWS, :] * w_ref[j - 1:j, cols]
            y_ref[:, cols] = acc + b_ref[:, cols]
        y = y_ref[...]
        yc = y - jnp.mean(y, axis=-1, keepdims=True)
        n = yc * lax.rsqrt(jnp.mean(yc * yc, axis=-1, keepdims=True) + EPS)
        n = n * lg_ref[...] + lb_ref[...]
        o_ref[pl.ds(r0, CONV_ROWS), :] = (n * jax.nn.sigmoid(n)).astype(o_ref.dtype)
        return carry

    lax.fori_loop(0, seq // CONV_ROWS, step, 0)


def _conformer_conv(u, conv_w, conv_b, ln_g, ln_b, *, seq, row0):
    blk0 = row0 // seq
    vec = pl.BlockSpec((1, CONV_W), lambda b: (0, 0))
    return pl.pallas_call(
        functools.partial(_conv_kernel, seq=seq),
        grid=(BATCH,),
        in_specs=[
            pl.BlockSpec((seq, CONV_W), lambda b: (blk0 + b, U_A // CONV_W)),
            pl.BlockSpec((seq, CONV_W), lambda b: (blk0 + b, U_G // CONV_W)),
            pl.BlockSpec((CONV_K, CONV_W), lambda b: (0, 0)),
            vec, vec, vec,
        ],
        out_specs=pl.BlockSpec((seq, CONV_W), lambda b: (b, 0)),
        out_shape=jax.ShapeDtypeStruct((BATCH * seq, CONV_W), BF16),
        scratch_shapes=[
            pltpu.VMEM((seq + 2 * CONV_PAD, CONV_W), F32),
            pltpu.VMEM((CONV_ROWS, CONV_W), F32),
        ],
        compiler_params=_cparams(("parallel",)),
        name="conformer_conv_%d" % seq,
    )(u, u, conv_w, conv_b, ln_g, ln_b)


def _rotate(blk, table):
    t = blk * table
    return t + pltpu.roll(t, QK_ROPE, axis=1)


def _qproj_kernel(cq_ref, g_ref, t_ref, w_ref, q_ref, cn_ref, *, heads_per_step):
    @pl.when(pl.program_id(1) == 0)
    def _():
        cn_ref[...] = _rms(cq_ref[...].astype(F32), g_ref[...]).astype(BF16)

    acc = jnp.dot(cn_ref[...], w_ref[...], preferred_element_type=F32) * ATTN_SCALE
    table = t_ref[...]
    for h in range(heads_per_step):
        lo = h * HEAD_W
        q_ref[:, lo:lo + QK_NOPE] = acc[:, lo:lo + QK_NOPE].astype(BF16)
        q_ref[:, lo + QK_NOPE:lo + HEAD_W] = _rotate(
            acc[:, lo + QK_NOPE:lo + HEAD_W], table).astype(BF16)


def _q_proj(u, g, rope, w):
    tm = 512
    hps = 4
    tn = hps * HEAD_W
    return pl.pallas_call(
        functools.partial(_qproj_kernel, heads_per_step=hps),
        grid=(ROWS // tm, MLA_HEADS // hps),
        in_specs=[
            pl.BlockSpec((tm, Q_LORA), lambda i, j: (i, U_Q // Q_LORA)),
            pl.BlockSpec((1, Q_LORA), lambda i, j: (0, 0)),
            pl.BlockSpec((tm, LANES), lambda i, j: (_rope_block_of_tile(i, tm), 0)),
            pl.BlockSpec((Q_LORA, tn), lambda i, j: (0, j)),
        ],
        out_specs=pl.BlockSpec((tm, tn), lambda i, j: (i, j)),
        out_shape=jax.ShapeDtypeStruct((ROWS, MLA_HEADS * HEAD_W), BF16),
        scratch_shapes=[pltpu.VMEM((tm, Q_LORA), BF16)],
        compiler_params=_cparams(("parallel", "arbitrary")),
        name="q_proj",
    )(u, g, rope, w)


def _kvproj_kernel(ckv_ref, kr_ref, t_ref, g_ref, wk_ref, wv_ref, k_ref, v_ref):
    cn = _rms(ckv_ref[...].astype(F32), g_ref[...]).astype(BF16)
    kn = jnp.dot(cn, wk_ref[...], preferred_element_type=F32)
    r = _rotate(kr_ref[...].astype(F32), t_ref[...])
    lane = lax.broadcasted_iota(jnp.int32, r.shape, 1)
    kr = jnp.where(lane < QK_ROPE, r, 0.0).astype(BF16)
    for h in range(MLA_HEADS):
        k_ref[:, h * HEAD_W:h * HEAD_W + QK_NOPE] = kn[:, h * QK_NOPE:(h + 1) * QK_NOPE].astype(BF16)
        k_ref[:, h * HEAD_W + QK_NOPE:(h + 1) * HEAD_W] = kr
    v_ref[...] = jnp.dot(cn, wv_ref[...], preferred_element_type=F32).astype(BF16)


def _kv_proj(u, g, rope, w):
    tm = 512
    nk = MLA_HEADS * QK_NOPE
    return pl.pallas_call(
        _kvproj_kernel,
        grid=(ROWS // tm,),
        in_specs=[
            pl.BlockSpec((tm, KV_LORA), lambda i: (i, U_CKV // KV_LORA)),
            pl.BlockSpec((tm, LANES), lambda i: (i, U_KR // LANES)),
            pl.BlockSpec((tm, LANES), lambda i: (_rope_block_of_tile(i, tm), 0)),
            pl.BlockSpec((1, KV_LORA), lambda i: (0, 0)),
            pl.BlockSpec((KV_LORA, nk), lambda i: (0, 0)),
            pl.BlockSpec((KV_LORA, MLA_W), lambda i: (0, nk // MLA_W)),
        ],
        out_specs=[
            pl.BlockSpec((tm, MLA_HEADS * HEAD_W), lambda i: (i, 0)),
            pl.BlockSpec((tm, MLA_W), lambda i: (i, 0)),
        ],
        out_shape=[
            jax.ShapeDtypeStruct((ROWS, MLA_HEADS * HEAD_W), BF16),
            jax.ShapeDtypeStruct((ROWS, MLA_W), BF16),
        ],
        compiler_params=_cparams(("parallel",)),
        name="kv_proj",
    )(u, u, rope, g, w, w)


_NT = (((1,), (1,)), ((), ()))


def _attn_latent_kernel(q_ref, kx_ref, kc_ref, vx_ref, vc_ref, o_ref):
    q = q_ref[...]
    s1 = lax.dot_general(q, kx_ref[...], _NT, preferred_element_type=F32)
    s2 = lax.dot_general(q, kc_ref[...], _NT, preferred_element_type=F32)
    m = jnp.maximum(jnp.max(s1, axis=-1, keepdims=True), jnp.max(s2, axis=-1, keepdims=True))
    p1 = jnp.exp(s1 - m)
    p2 = jnp.exp(s2 - m)
    l = jnp.sum(p1, axis=-1, keepdims=True) + jnp.sum(p2, axis=-1, keepdims=True)
    o = jnp.dot(p1.astype(BF16), vx_ref[...], preferred_element_type=F32)
    o = o + jnp.dot(p2.astype(BF16), vc_ref[...], preferred_element_type=F32)
    o_ref[...] = (o / l).astype(o_ref.dtype)


def _attn_latent(q, k, v):
    tq = 1024
    nq = SEQ // tq
    cblk = ROWS_X // CTX_LEN
    return pl.pallas_call(
        _attn_latent_kernel,
        grid=(BATCH, MLA_HEADS, nq),
        in_specs=[
            pl.BlockSpec((tq, HEAD_W), lambda b, h, i: (b * nq + i, h)),
            pl.BlockSpec((SEQ, HEAD_W), lambda b, h, i: (b, h)),
            pl.BlockSpec((CTX_LEN, HEAD_W), lambda b, h, i: (cblk + b, h)),
            pl.BlockSpec((SEQ, V_DIM), lambda b, h, i: (b, h)),
            pl.BlockSpec((CTX_LEN, V_DIM), lambda b, h, i: (cblk + b, h)),
        ],
        out_specs=pl.BlockSpec((tq, V_DIM), lambda b, h, i: (b * nq + i, h)),
        out_shape=jax.ShapeDtypeStruct((ROWS_X, MLA_W), BF16),
        compiler_params=_cparams(("parallel", "parallel", "arbitrary")),
        name="attn_latent",
    )(q, k, k, v, v)


def _attn_ctx_kernel(q_ref, k_ref, v_ref, o_ref):
    s = lax.dot_general(q_ref[...], k_ref[...], _NT, preferred_element_type=F32)
    p = jnp.exp(s - jnp.max(s, axis=-1, keepdims=True))
    l = jnp.sum(p, axis=-1, keepdims=True)
    o = jnp.dot(p.astype(BF16), v_ref[...], preferred_element_type=F32)
    o_ref[...] = (o / l).astype(o_ref.dtype)


def _attn_ctx(q, k, v):
    cblk = ROWS_X // CTX_LEN
    return pl.pallas_call(
        _attn_ctx_kernel,
        grid=(BATCH, MLA_HEADS),
        in_specs=[
            pl.BlockSpec((CTX_LEN, HEAD_W), lambda b, h: (cblk + b, h)),
            pl.BlockSpec((CTX_LEN, HEAD_W), lambda b, h: (cblk + b, h)),
            pl.BlockSpec((CTX_LEN, V_DIM), lambda b, h: (cblk + b, h)),
        ],
        out_specs=pl.BlockSpec((CTX_LEN, V_DIM), lambda b, h: (b, h)),
        out_shape=jax.ShapeDtypeStruct((ROWS_C, MLA_W), BF16),
        compiler_params=_cparams(("parallel", "parallel")),
        name="attn_ctx",
    )(q, k, v)


def _merge_kernel(f_ref, c_ref, m_ref, hg_ref, wf_ref, wc_ref, wm_ref,
                  gf_ref, gc_ref, gm_ref, bf_ref, bc_ref, bm_ref, y_ref):
    hg = hg_ref[...]

    def gated(x_ref, w_ref, wg_ref, bg_ref):
        gate = jax.nn.sigmoid(jnp.dot(hg, wg_ref[...], preferred_element_type=F32) + bg_ref[...])
        return gate * jnp.dot(x_ref[...], w_ref[...], preferred_element_type=F32)

    y = gated(f_ref, wf_ref, gf_ref, bf_ref)
    y = y + gated(c_ref, wc_ref, gc_ref, bc_ref)
    y = y + gated(m_ref, wm_ref, gm_ref, bm_ref)
    y_ref[...] = y.astype(y_ref.dtype)


def _merge(f, cv, o, u, w_pf, w_pc, w_pm, w_gb, b_g):
    tm = 512
    tn = 512
    nb = D_MODEL // tn

    def gate_w(k):
        return pl.BlockSpec((GATE_RANK, tn), lambda i, j: (0, k * nb + j))

    def gate_b(k):
        return pl.BlockSpec((1, tn), lambda i, j: (0, k * nb + j))

    return pl.pallas_call(
        _merge_kernel,
        grid=(ROWS // tm, nb),
        in_specs=[
            pl.BlockSpec((tm, FOURIER_W), lambda i, j: (i, 0)),
            pl.BlockSpec((tm, CONV_W), lambda i, j: (i, 0)),
            pl.BlockSpec((tm, MLA_W), lambda i, j: (i, 0)),
            pl.BlockSpec((tm, GATE_RANK), lambda i, j: (i, U_HG // GATE_RANK)),
            pl.BlockSpec((FOURIER_W, tn), lambda i, j: (0, j)),
            pl.BlockSpec((CONV_W, tn), lambda i, j: (0, j)),
            pl.BlockSpec((MLA_W, tn), lambda i, j: (0, j)),
            gate_w(0), gate_w(1), gate_w(2),
            gate_b(0), gate_b(1), gate_b(2),
        ],
        out_specs=pl.BlockSpec((tm, tn), lambda i, j: (i, j)),
        out_shape=jax.ShapeDtypeStruct((ROWS, D_MODEL), BF16),
        compiler_params=_cparams(("parallel", "parallel")),
        name="merge",
    )(f, cv, o, u, w_pf, w_pc, w_pm, w_gb, w_gb, w_gb, b_g, b_g, b_g)


_ROPE_SWAP = tuple(
    ax * 2 * ROPE_FREQS + (1 - half) * ROPE_FREQS + f
    for ax in range(2) for half in range(2) for f in range(ROPE_FREQS))


def _rope_table(tm):
    n_rows = SEQ // GRID_W
    rows = jnp.repeat(jnp.arange(n_rows, dtype=F32), GRID_W)
    cols = jnp.tile(jnp.arange(GRID_W, dtype=F32), n_rows)
    pos = jnp.stack([rows, cols], axis=-1)
    inv_freq = jnp.power(ROPE_THETA, -jnp.arange(ROPE_FREQS, dtype=F32) / ROPE_FREQS)
    ang = pos[:, :, None] * inv_freq
    cos, sin = jnp.cos(ang), jnp.sin(ang)
    cos64 = jnp.stack([cos, cos], axis=2).reshape(SEQ, QK_ROPE)
    sin64 = jnp.stack([-sin, sin], axis=2).reshape(SEQ, QK_ROPE)
    latent = jnp.concatenate([cos64, sin64], axis=-1)
    ident = jnp.concatenate([jnp.ones((tm, QK_ROPE), F32), jnp.zeros((tm, QK_ROPE), F32)], axis=-1)
    return jnp.concatenate([latent, ident], axis=0)


def _prep_layer_weights(i, w_in, w_gate_a, w_uq, w_ukv):
    swap = jnp.array(_ROPE_SWAP, dtype=jnp.int32)
    wi = w_in[i]
    kr = wi[:, OFF_KV + KV_LORA:OFF_KV + KV_LORA + QK_ROPE]
    w_u = jnp.concatenate([
        wi[:, :OFF_C],
        wi[:, OFF_C:OFF_C + CONV_W],
        wi[:, OFF_C + CONV_W:OFF_Q],
        wi[:, OFF_Q:OFF_KV],
        w_gate_a[i],
        wi[:, OFF_KV:OFF_KV + KV_LORA],
        kr,
        kr[:, swap],
        jnp.zeros((D_MODEL, U_W - U_KR - 2 * QK_ROPE), F32),
    ], axis=1).astype(BF16)
    wq = w_uq[i].reshape(Q_LORA, MLA_HEADS, QK_NOPE + QK_ROPE)
    qr = wq[:, :, QK_NOPE:]
    w_q = jnp.concatenate([wq[:, :, :QK_NOPE], qr, qr[:, :, swap]], axis=-1)
    w_q = w_q.reshape(Q_LORA, MLA_HEADS * HEAD_W).astype(BF16)
    wkv = w_ukv[i].reshape(KV_LORA, MLA_HEADS, QK_NOPE + V_DIM)
    w_kv = jnp.concatenate([
        wkv[:, :, :QK_NOPE].reshape(KV_LORA, MLA_HEADS * QK_NOPE),
        wkv[:, :, QK_NOPE:].reshape(KV_LORA, MLA_W),
    ], axis=1).astype(BF16)
    return w_u, w_q, w_kv


def kernel(x, c, ctx, c_ctx, g_mix_pre, g_mix_post, g_mlp_pre, g_mlp_post, w_mod_a, w_mod_b, b_mod, w_in, conv_w, conv_b, conv_ln_g, conv_ln_b, q_norm_g, w_uq, kv_norm_g, w_ukv, w_pf, w_pc, w_pm, w_gate_a, w_gate_b, b_gate, w_out, w_ff1, w_ff2):
    rope_tm = 512
    rope = _rope_table(rope_tm)

    gw = FOURIER_GROUP_W
    cc, sc = _dft_tables(gw)
    wc = (cc * (1.0 / 16.0)).astype(BF16)
    ws = (sc * (1.0 / 16.0)).astype(BF16)
    clx, slx = _dft_tables(SEQ)
    clx, slx = clx.astype(BF16), (-slx).astype(BF16)
    clc, slc = _dft_tables(CTX_LEN)
    clc, slc = clc.astype(BF16), (-slc).astype(BF16)

    cond = jnp.concatenate(
        [c, c_ctx[None, :], jnp.zeros((N_GROUPS - BATCH - 1, D_MODEL), F32)], axis=0)
    mod_all = _adaln(cond, w_mod_a.astype(BF16), w_mod_b.astype(BF16), b_mod[:, None, :])
    mod_all = mod_all.reshape(DEPTH, N_GROUPS, N_MOD, D_MODEL).transpose(0, 2, 1, 3)[:, :, :, None, :]

    t = jnp.concatenate([x.reshape(ROWS_X, D_MODEL), ctx.reshape(ROWS_C, D_MODEL)], axis=0)

    for i in range(DEPTH):
        mod = mod_all[i]
        w_u, w_q, w_kv = _prep_layer_weights(i, w_in, w_gate_a, w_uq, w_ukv)

        h = _modulate(t, g_mix_pre[i][None, :], mod, 0, 1)
        u = _matmul(h, w_u, tm=1024, tn=768, tk=D_MODEL, out_dtype=BF16, name="in_proj")

        ab = _channel_dft(u, wc, ws)
        f = jnp.concatenate([
            _seq_dft(ab, clx, slx, seq=SEQ, row0=0),
            _seq_dft(ab, clc, slc, seq=CTX_LEN, row0=ROWS_X),
        ], axis=0)

        conv_args = (conv_w[i], conv_b[i][None, :], conv_ln_g[i][None, :], conv_ln_b[i][None, :])
        cv = jnp.concatenate([
            _conformer_conv(u, *conv_args, seq=SEQ, row0=0),
            _conformer_conv(u, *conv_args, seq=CTX_LEN, row0=ROWS_X),
        ], axis=0)

        q = _q_proj(u, q_norm_g[i][None, :], rope, w_q)
        k, v = _kv_proj(u, kv_norm_g[i][None, :], rope, w_kv)
        o = jnp.concatenate([_attn_latent(q, k, v), _attn_ctx(q, k, v)], axis=0)

        y = _merge(f, cv, o, u, w_pf[i].astype(BF16), w_pc[i].astype(BF16), w_pm[i].astype(BF16),
                   w_gate_b[i].astype(BF16), b_gate[i][None, :])
        yo = _matmul(y, w_out[i].astype(BF16), tm=1024, tn=1024, tk=D_MODEL, out_dtype=F32,
                     name="out_proj")
        t = _residual(t, yo, g_mix_post[i][None, :], mod, 2)

        h = _modulate(t, g_mlp_pre[i][None, :], mod, 3, 4)
        hid = _matmul(h, w_ff1[i].astype(BF16), tm=1024, tn=1024, tk=D_MODEL, out_dtype=BF16,
                      sq_relu=True, name="ff1")
        yo = _matmul(hid, w_ff2[i].astype(BF16), tm=1024, tn=1024, tk=2048, out_dtype=F32,
                     name="ff2")
        t = _residual(t, yo, g_mlp_post[i][None, :], mod, 5)

    return t[:ROWS_X].reshape(BATCH, SEQ, D_MODEL)
```

```python
import functools
import math

import jax
import jax.numpy as jnp
from jax import lax
from jax.experimental import pallas as pl
from jax.experimental.pallas import tpu as pltpu

F32 = jnp.float32
BF16 = jnp.bfloat16

D_MODEL = 4096
BATCH = 4
SEQ = 2048
DEPTH = 4
GRID_W = 64
CTX_LEN = 256
FOURIER_GROUPS = 4
FOURIER_GROUP_W = D_MODEL // 16
FOURIER_W = FOURIER_GROUPS * FOURIER_GROUP_W
CONV_W = D_MODEL // 4
CONV_K = 31
MLA_HEADS = D_MODEL // 256
Q_LORA = D_MODEL // 4
KV_LORA = D_MODEL // 8
QK_NOPE = 128
QK_ROPE = 64
V_DIM = 128
MLA_W = MLA_HEADS * V_DIM
ROPE_THETA = 10000.0
ROPE_FREQS = QK_ROPE // 4
ATTN_SCALE = (QK_NOPE + QK_ROPE) ** -0.5
Q_SCALE = ATTN_SCALE * math.log2(math.e)
N_BRANCH = 3
GATE_RANK = 512
MOD_RANK = 256
N_MOD = 6
D_FF = 4 * D_MODEL
EPS = 1e-6
OFF_C = FOURIER_W
OFF_Q = OFF_C + 2 * CONV_W
OFF_KV = OFF_Q + Q_LORA

ROWS_X = BATCH * SEQ
ROWS_C = BATCH * CTX_LEN
ROWS = ROWS_X + ROWS_C
N_GROUPS = 8
CTX_GROUP = BATCH
HEAD_W = 2 * QK_NOPE
CONV_PAD = 16
LANES = 128

U_F = 0
U_A = U_F + FOURIER_W
U_G = U_A + CONV_W
U_Q = U_G + CONV_W
U_HG = U_Q + Q_LORA
U_CKV = U_HG + GATE_RANK
U_KR = U_CKV + KV_LORA
U_W = 5376

VMEM_LIMIT = 56 * 1024 * 1024


def _cparams(sem, limit=VMEM_LIMIT):
    return pltpu.CompilerParams(dimension_semantics=sem, vmem_limit_bytes=limit)


def _group_of_tile(i, tm):
    return jnp.where(i < ROWS_X // tm, i // (SEQ // tm), CTX_GROUP)


def _rope_block_of_tile(i, tm):
    return jnp.where(i < ROWS_X // tm, i % (SEQ // tm), SEQ // tm)


def _rms(x, g):
    return x * lax.rsqrt(jnp.mean(x * x, axis=-1, keepdims=True) + EPS) * g


def _mod_spec(layer, k, tm):
    return pl.BlockSpec((None, None, None, 1, D_MODEL),
                        lambda i: (layer, k, _group_of_tile(i, tm), 0, 0))


def _gain_spec(layer, width=D_MODEL):
    return pl.BlockSpec((None, 1, width), lambda *_: (layer, 0, 0))


def _adaln_kernel(c_ref, wa_ref, wb_ref, b_ref, o_ref):
    c = c_ref[...]
    s = (c * jax.nn.sigmoid(c)).astype(BF16)
    t = jnp.dot(s, wa_ref[...], preferred_element_type=F32)
    o_ref[...] = jnp.dot(t.astype(BF16), wb_ref[...], preferred_element_type=F32) + b_ref[...]


def _adaln(cond, w_a, w_b, b):
    tn = D_MODEL
    n = N_MOD * D_MODEL
    return pl.pallas_call(
        _adaln_kernel,
        grid=(DEPTH, n // tn),
        in_specs=[
            pl.BlockSpec((N_GROUPS, D_MODEL), lambda l, j: (0, 0)),
            pl.BlockSpec((None, D_MODEL, MOD_RANK), lambda l, j: (l, 0, 0)),
            pl.BlockSpec((None, MOD_RANK, tn), lambda l, j: (l, 0, j)),
            pl.BlockSpec((None, 1, tn), lambda l, j: (l, 0, j)),
        ],
        out_specs=pl.BlockSpec((None, N_GROUPS, tn), lambda l, j: (l, 0, j)),
        out_shape=jax.ShapeDtypeStruct((DEPTH, N_GROUPS, n), F32),
        compiler_params=_cparams(("parallel", "parallel")),
        name="adaln",
    )(cond, w_a, w_b, b)


ROW_TM = 256


def _modulated(x, g_ref, sh_ref, sc_ref):
    return (_rms(x, g_ref[...]) * (1.0 + sc_ref[...]) + sh_ref[...]).astype(BF16)


def _modulate_kernel(x_ref, g_ref, sh_ref, sc_ref, o_ref):
    o_ref[...] = _modulated(x_ref[...], g_ref, sh_ref, sc_ref)


def _modulate(x, g, mod, layer, k_shift, k_scale):
    tm = ROW_TM
    row = pl.BlockSpec((tm, D_MODEL), lambda i: (i, 0))
    return pl.pallas_call(
        _modulate_kernel,
        grid=(ROWS // tm,),
        in_specs=[row, _gain_spec(layer), _mod_spec(layer, k_shift, tm), _mod_spec(layer, k_scale, tm)],
        out_specs=row,
        out_shape=jax.ShapeDtypeStruct((ROWS, D_MODEL), BF16),
        compiler_params=_cparams(("parallel",)),
        name="modulate",
    )(x, g, mod, mod)


def _residual_kernel(x_ref, y_ref, g_ref, gate_ref, o_ref):
    o_ref[...] = x_ref[...] + gate_ref[...] * _rms(y_ref[...].astype(F32), g_ref[...])


def _residual(x, y, g, mod, layer, k_gate, n_rows):
    tm = ROW_TM
    row = pl.BlockSpec((tm, D_MODEL), lambda i: (i, 0))
    return pl.pallas_call(
        _residual_kernel,
        grid=(n_rows // tm,),
        in_specs=[row, row, _gain_spec(layer), _mod_spec(layer, k_gate, tm)],
        out_specs=row,
        out_shape=jax.ShapeDtypeStruct((n_rows, D_MODEL), F32),
        compiler_params=_cparams(("parallel",)),
        name="residual",
    )(x, y, g, mod)


def _resmod_kernel(x_ref, y_ref, gpost_ref, gate_ref, gpre_ref, sh_ref, sc_ref, xo_ref, h_ref):
    xn = x_ref[...] + gate_ref[...] * _rms(y_ref[...].astype(F32), gpost_ref[...])
    xo_ref[...] = xn
    h_ref[...] = _modulated(xn, gpre_ref, sh_ref, sc_ref)


def _residual_modulate(x, y, g_post, mod, layer, k_gate, g_pre, layer_pre, k_shift, k_scale, n_rows):
    tm = ROW_TM
    row = pl.BlockSpec((tm, D_MODEL), lambda i: (i, 0))
    return pl.pallas_call(
        _resmod_kernel,
        grid=(n_rows // tm,),
        in_specs=[row, row, _gain_spec(layer), _mod_spec(layer, k_gate, tm),
                  _gain_spec(layer_pre), _mod_spec(layer_pre, k_shift, tm),
                  _mod_spec(layer_pre, k_scale, tm)],
        out_specs=[row, row],
        out_shape=[jax.ShapeDtypeStruct((n_rows, D_MODEL), F32),
                   jax.ShapeDtypeStruct((n_rows, D_MODEL), BF16)],
        compiler_params=_cparams(("parallel",)),
        name="residual_modulate",
    )(x, y, g_post, mod, g_pre, mod, mod)


def _mm_kernel(a_ref, w_ref, o_ref, *scratch, nk, sq_relu):
    def finish(r):
        if sq_relu:
            r = jnp.square(jnp.maximum(r, 0.0))
        o_ref[...] = r.astype(o_ref.dtype)

    p = jnp.dot(a_ref[...], w_ref[...], preferred_element_type=F32)
    if nk == 1:
        finish(p)
        return
    acc_ref, = scratch
    k = pl.program_id(2)

    @pl.when(k == 0)
    def _():
        acc_ref[...] = p

    @pl.when(k > 0)
    def _():
        acc_ref[...] += p

    @pl.when(k == nk - 1)
    def _():
        finish(acc_ref[...])


def _matmul(a, w, layer, *, n_rows, tm, tn, tk, out_dtype=BF16, sq_relu=False, name):
    kdim, n = w.shape[1:]
    nk = kdim // tk
    return pl.pallas_call(
        functools.partial(_mm_kernel, nk=nk, sq_relu=sq_relu),
        grid=(n_rows // tm, n // tn, nk),
        in_specs=[
            pl.BlockSpec((tm, tk), lambda i, j, k: (i, k)),
            pl.BlockSpec((None, tk, tn), lambda i, j, k: (layer, k, j)),
        ],
        out_specs=pl.BlockSpec((tm, tn), lambda i, j, k: (i, j)),
        out_shape=jax.ShapeDtypeStruct((n_rows, n), out_dtype),
        scratch_shapes=[pltpu.VMEM((tm, tn), F32)] if nk > 1 else [],
        compiler_params=_cparams(("parallel", "parallel", "arbitrary")),
        name=name,
    )(a, w)


def _chdft_kernel(u_ref, wc_ref, ws_ref, ab_ref):
    gw = FOURIER_GROUP_W
    for g in range(FOURIER_GROUPS):
        ug = u_ref[:, g * gw:(g + 1) * gw]
        ab_ref[:, g * gw:(g + 1) * gw] = jnp.dot(
            ug, wc_ref[...], preferred_element_type=F32).astype(BF16)
        ab_ref[:, FOURIER_W + g * gw:FOURIER_W + (g + 1) * gw] = jnp.dot(
            ug, ws_ref[...], preferred_element_type=F32).astype(BF16)


def _channel_dft(u, wc, ws, n_rows):
    tm = 1024
    gw = FOURIER_GROUP_W
    return pl.pallas_call(
        _chdft_kernel,
        grid=(n_rows // tm,),
        in_specs=[
            pl.BlockSpec((tm, FOURIER_W), lambda i: (i, U_F // FOURIER_W)),
            pl.BlockSpec((gw, gw), lambda i: (0, 0)),
            pl.BlockSpec((gw, gw), lambda i: (0, 0)),
        ],
        out_specs=pl.BlockSpec((tm, 2 * FOURIER_W), lambda i: (i, 0)),
        out_shape=jax.ShapeDtypeStruct((n_rows, 2 * FOURIER_W), BF16),
        compiler_params=_cparams(("parallel",)),
        name="channel_dft",
    )(u, wc, ws)


def _seqdft_kernel(cl_ref, sl_ref, a_ref, b_ref, o_ref, *, scale):
    r = jnp.dot(cl_ref[...], a_ref[...], preferred_element_type=F32)
    r = r + jnp.dot(sl_ref[...], b_ref[...], preferred_element_type=F32)
    o_ref[...] = (r * scale).astype(o_ref.dtype)


def _seq_dft(ab, cl, sl, *, seq, row0):
    tm = min(seq, 1024)
    nm = seq // tm
    blk0 = row0 // seq
    return pl.pallas_call(
        functools.partial(_seqdft_kernel, scale=float(seq) ** -0.5),
        grid=(nm, BATCH),
        in_specs=[
            pl.BlockSpec((tm, seq), lambda m, b: (m, 0)),
            pl.BlockSpec((tm, seq), lambda m, b: (m, 0)),
            pl.BlockSpec((seq, FOURIER_W), lambda m, b: (blk0 + b, 0)),
            pl.BlockSpec((seq, FOURIER_W), lambda m, b: (blk0 + b, 1)),
        ],
        out_specs=pl.BlockSpec((tm, FOURIER_W), lambda m, b: (b * nm + m, 0)),
        out_shape=jax.ShapeDtypeStruct((BATCH * seq, FOURIER_W), BF16),
        compiler_params=_cparams(("parallel", "parallel")),
        name="seq_dft_%d" % seq,
    )(cl, sl, ab, ab)


def _dft_tables(n):
    idx = jnp.arange(n, dtype=jnp.int32)
    ang = ((idx[:, None] * idx[None, :]) % n).astype(F32) * (2.0 * jnp.pi / n)
    return jnp.cos(ang), jnp.sin(ang)


CONV_ROWS = 64


def _conv_kernel(a_ref, gt_ref, w_ref, b_ref, lg_ref, lb_ref, o_ref, vpad_ref, y_ref, *, seq):
    zeros = jnp.zeros((CONV_PAD, CONV_W), F32)
    vpad_ref[0:CONV_PAD, :] = zeros
    vpad_ref[CONV_PAD + seq:2 * CONV_PAD + seq, :] = zeros
    glu_rows = 256

    def glu(i, carry):
        r0 = pl.multiple_of(i * glu_rows, glu_rows)
        a = a_ref[pl.ds(r0, glu_rows), :].astype(F32)
        g = gt_ref[pl.ds(r0, glu_rows), :].astype(F32)
        vpad_ref[pl.ds(r0 + CONV_PAD, glu_rows), :] = a * jax.nn.sigmoid(g)
        return carry

    lax.fori_loop(0, seq // glu_rows, glu, 0)

    win_rows = CONV_ROWS + 2 * CONV_PAD

    def step(i, carry):
        r0 = pl.multiple_of(i * CONV_ROWS, CONV_ROWS)
        for c in range(CONV_W // LANES):
            cols = slice(c * LANES, (c + 1) * LANES)
            win = vpad_ref[pl.ds(r0, win_rows), cols]
            acc = jnp.zeros((CONV_ROWS, LANES), F32)
            for b in range(8):
                wb = win if b == 0 else pltpu.roll(win, win_rows - b, axis=0)
                for a in range(4):
                    j = 8 * a + b
                    if j == 0:
                        continue
                    acc = acc + wb[8 * a:8 * a + CONV_ROWS, :] * w_ref[j - 1:j, cols]
            y_ref[:, cols] = acc + b_ref[:, cols]
        y = y_ref[...]
        yc = y - jnp.mean(y, axis=-1, keepdims=True)
        n = yc * lax.rsqrt(jnp.mean(yc * yc, axis=-1, keepdims=True) + EPS)
        n = n * lg_ref[...] + lb_ref[...]
        o_ref[pl.ds(r0, CONV_ROWS), :] = (n * jax.nn.sigmoid(n)).astype(o_ref.dtype)
        return carry

    lax.fori_loop(0, seq // CONV_ROWS, step, 0)


def _conformer_conv(u, conv_w, conv_b, ln_g, ln_b, layer, *, seq, row0):
    blk0 = row0 // seq
    vec = _gain_spec(layer, CONV_W)
    return pl.pallas_call(
        functools.partial(_conv_kernel, seq=seq),
        grid=(BATCH,),
        in_specs=[
            pl.BlockSpec((seq, CONV_W), lambda b: (blk0 + b, U_A // CONV_W)),
            pl.BlockSpec((seq, CONV_W), lambda b: (blk0 + b, U_G // CONV_W)),
            pl.BlockSpec((None, CONV_K, CONV_W), lambda b: (layer, 0, 0)),
            vec, vec, vec,
        ],
        out_specs=pl.BlockSpec((seq, CONV_W), lambda b: (b, 0)),
        out_shape=jax.ShapeDtypeStruct((BATCH * seq, CONV_W), BF16),
        scratch_shapes=[
            pltpu.VMEM((seq + 2 * CONV_PAD, CONV_W), F32),
            pltpu.VMEM((CONV_ROWS, CONV_W), F32),
        ],
        compiler_params=_cparams(("parallel",)),
        name="conformer_conv_%d" % seq,
    )(u, u, conv_w, conv_b, ln_g, ln_b)


PROJ_TM = 512


def _rotate(blk, table):
    t = blk * table
    return t + pltpu.roll(t, QK_ROPE, axis=1)


def _qproj_kernel(cq_ref, g_ref, t_ref, w_ref, q_ref, cn_ref, *, heads_per_step):
    @pl.when(pl.program_id(1) == 0)
    def _():
        cn_ref[...] = _rms(cq_ref[...].astype(F32), g_ref[...]).astype(BF16)

    acc = jnp.dot(cn_ref[...], w_ref[...], preferred_element_type=F32) * Q_SCALE
    table = t_ref[...]
    for h in range(heads_per_step):
        lo = h * HEAD_W
        q_ref[:, lo:lo + QK_NOPE] = acc[:, lo:lo + QK_NOPE].astype(BF16)
        q_ref[:, lo + QK_NOPE:lo + HEAD_W] = _rotate(
            acc[:, lo + QK_NOPE:lo + HEAD_W], table).astype(BF16)


def _q_proj(u, g, rope, w, layer, n_rows):
    tm = PROJ_TM
    hps = 4
    tn = hps * HEAD_W
    return pl.pallas_call(
        functools.partial(_qproj_kernel, heads_per_step=hps),
        grid=(n_rows // tm, MLA_HEADS // hps),
        in_specs=[
            pl.BlockSpec((tm, Q_LORA), lambda i, j: (i, U_Q // Q_LORA)),
            _gain_spec(layer, Q_LORA),
            pl.BlockSpec((tm, LANES), lambda i, j: (_rope_block_of_tile(i, tm), 0)),
            pl.BlockSpec((None, Q_LORA, tn), lambda i, j: (layer, 0, j)),
        ],
        out_specs=pl.BlockSpec((tm, tn), lambda i, j: (i, j)),
        out_shape=jax.ShapeDtypeStruct((n_rows, MLA_HEADS * HEAD_W), BF16),
        scratch_shapes=[pltpu.VMEM((tm, Q_LORA), BF16)],
        compiler_params=_cparams(("parallel", "arbitrary")),
        name="q_proj",
    )(u, g, rope, w)


def _kvproj_kernel(ckv_ref, kr_ref, t_ref, g_ref, wk_ref, wv_ref, k_ref, v_ref):
    cn = _rms(ckv_ref[...].astype(F32), g_ref[...]).astype(BF16)
    kn = jnp.dot(cn, wk_ref[...], preferred_element_type=F32)
    r = _rotate(kr_ref[...].astype(F32), t_ref[...])
    lane = lax.broadcasted_iota(jnp.int32, r.shape, 1)
    kr = jnp.where(lane < QK_ROPE, r, 0.0).astype(BF16)
    for h in range(MLA_HEADS):
        k_ref[:, h * HEAD_W:h * HEAD_W + QK_NOPE] = kn[:, h * QK_NOPE:(h + 1) * QK_NOPE].astype(BF16)
        k_ref[:, h * HEAD_W + QK_NOPE:(h + 1) * HEAD_W] = kr
    v = jnp.dot(cn, wv_ref[...], preferred_element_type=F32)
    ones = jnp.ones((v.shape[0], V_DIM), BF16)
    for h in range(MLA_HEADS):
        v_ref[:, h * HEAD_W:h * HEAD_W + V_DIM] = v[:, h * V_DIM:(h + 1) * V_DIM].astype(BF16)
        v_ref[:, h * HEAD_W + V_DIM:(h + 1) * HEAD_W] = ones


def _kv_proj(u, g, rope, w, layer):
    tm = PROJ_TM
    nk = MLA_HEADS * QK_NOPE
    slot = pl.BlockSpec((tm, MLA_HEADS * HEAD_W), lambda i: (i, 0))
    shape = jax.ShapeDtypeStruct((ROWS, MLA_HEADS * HEAD_W), BF16)
    return pl.pallas_call(
        _kvproj_kernel,
        grid=(ROWS // tm,),
        in_specs=[
            pl.BlockSpec((tm, KV_LORA), lambda i: (i, U_CKV // KV_LORA)),
            pl.BlockSpec((tm, LANES), lambda i: (i, U_KR // LANES)),
            pl.BlockSpec((tm, LANES), lambda i: (_rope_block_of_tile(i, tm), 0)),
            _gain_spec(layer, KV_LORA),
            pl.BlockSpec((None, KV_LORA, nk), lambda i: (layer, 0, 0)),
            pl.BlockSpec((None, KV_LORA, MLA_W), lambda i: (layer, 0, nk // MLA_W)),
        ],
        out_specs=[slot, slot],
        out_shape=[shape, shape],
        compiler_params=_cparams(("parallel",)),
        name="kv_proj",
    )(u, u, rope, g, w, w)


_NT = (((1,), (1,)), ((), ()))
ATTN_TQ = 1024
ATTN_CHUNK = 256


def _softmax_pv(s_blocks, v_refs):
    m = functools.reduce(jnp.maximum, [jnp.max(s, axis=-1, keepdims=True) for s in s_blocks])
    o = None
    for s, v_ref in zip(s_blocks, v_refs):
        pv = jnp.dot(jnp.exp2(s - m).astype(BF16), v_ref[...], preferred_element_type=F32)
        o = pv if o is None else o + pv
    return o[:, :V_DIM] / o[:, V_DIM:]


def _attn_latent_kernel(q_ref, kx_ref, kc_ref, vx_ref, vc_ref, o_ref):
    for c in range(ATTN_TQ // ATTN_CHUNK):
        rows = slice(c * ATTN_CHUNK, (c + 1) * ATTN_CHUNK)
        q = q_ref[rows, :]
        s1 = lax.dot_general(q, kx_ref[...], _NT, preferred_element_type=F32)
        s2 = lax.dot_general(q, kc_ref[...], _NT, preferred_element_type=F32)
        o_ref[rows, :] = _softmax_pv([s1, s2], [vx_ref, vc_ref]).astype(o_ref.dtype)


def _attn_latent(q, k, v):
    tq = ATTN_TQ
    nq = SEQ // tq
    cblk = ROWS_X // CTX_LEN
    return pl.pallas_call(
        _attn_latent_kernel,
        grid=(BATCH, MLA_HEADS, nq),
        in_specs=[
            pl.BlockSpec((tq, HEAD_W), lambda b, h, i: (b * nq + i, h)),
            pl.BlockSpec((SEQ, HEAD_W), lambda b, h, i: (b, h)),
            pl.BlockSpec((CTX_LEN, HEAD_W), lambda b, h, i: (cblk + b, h)),
            pl.BlockSpec((SEQ, HEAD_W), lambda b, h, i: (b, h)),
            pl.BlockSpec((CTX_LEN, HEAD_W), lambda b, h, i: (cblk + b, h)),
        ],
        out_specs=pl.BlockSpec((tq, V_DIM), lambda b, h, i: (b * nq + i, h)),
        out_shape=jax.ShapeDtypeStruct((ROWS_X, MLA_W), BF16),
        compiler_params=_cparams(("parallel", "parallel", "arbitrary")),
        name="attn_latent",
    )(q, k, k, v, v)


def _attn_ctx_kernel(q_ref, k_ref, v_ref, o_ref):
    s = lax.dot_general(q_ref[...], k_ref[...], _NT, preferred_element_type=F32)
    o_ref[...] = _softmax_pv([s], [v_ref]).astype(o_ref.dtype)


def _attn_ctx(q, k, v):
    cblk = ROWS_X // CTX_LEN
    slot = pl.BlockSpec((CTX_LEN, HEAD_W), lambda b, h: (cblk + b, h))
    return pl.pallas_call(
        _attn_ctx_kernel,
        grid=(BATCH, MLA_HEADS),
        in_specs=[slot, slot, slot],
        out_specs=pl.BlockSpec((CTX_LEN, V_DIM), lambda b, h: (b, h)),
        out_shape=jax.ShapeDtypeStruct((ROWS_C, MLA_W), BF16),
        compiler_params=_cparams(("parallel", "parallel")),
        name="attn_ctx",
    )(q, k, v)


MERGE_TM = 512
MERGE_TN = 512


def _merge_kernel(*refs, with_ctx):
    n_act = 6 if with_ctx else 3
    acts = refs[:n_act]
    hg_ref, wf_ref, wc_ref, wm_ref, gf_ref, gc_ref, gm_ref, bf_ref, bc_ref, bm_ref, y_ref = refs[n_act:]
    if with_ctx:
        is_latent = pl.program_id(0) < ROWS_X // MERGE_TM
        f, c, m = (jnp.where(is_latent, acts[k][...], acts[k + 3][...]) for k in range(3))
    else:
        f, c, m = (r[...] for r in acts)
    hg = hg_ref[...]

    def gated(x, w_ref, wg_ref, bg_ref):
        gate = jax.nn.sigmoid(jnp.dot(hg, wg_ref[...], preferred_element_type=F32) + bg_ref[...])
        return gate * jnp.dot(x, w_ref[...], preferred_element_type=F32)

    y = gated(f, wf_ref, gf_ref, bf_ref) + gated(c, wc_ref, gc_ref, bc_ref) + gated(m, wm_ref, gm_ref, bm_ref)
    y_ref[...] = y.astype(y_ref.dtype)


def _merge(latent, ctx, u, w_pf, w_pc, w_pm, w_gb, b_g, layer):
    tm, tn = MERGE_TM, MERGE_TN
    nb = D_MODEL // tn
    nxt = ROWS_X // tm
    with_ctx = ctx is not None
    n_rows = ROWS if with_ctx else ROWS_X
    widths = (FOURIER_W, CONV_W, MLA_W)

    act_specs = [pl.BlockSpec((tm, w), lambda i, j: (jnp.minimum(i, nxt - 1), 0)) for w in widths]
    if with_ctx:
        act_specs += [pl.BlockSpec((tm, w), lambda i, j: (jnp.maximum(i - nxt, 0), 0)) for w in widths]

    def proj_w(w):
        return pl.BlockSpec((None, w, tn), lambda i, j: (layer, 0, j))

    def gate_w(k):
        return pl.BlockSpec((None, GATE_RANK, tn), lambda i, j: (layer, 0, k * nb + j))

    def gate_b(k):
        return pl.BlockSpec((None, 1, tn), lambda i, j: (layer, 0, k * nb + j))

    return pl.pallas_call(
        functools.partial(_merge_kernel, with_ctx=with_ctx),
        grid=(n_rows // tm, nb),
        in_specs=act_specs + [
            pl.BlockSpec((tm, GATE_RANK), lambda i, j: (i, U_HG // GATE_RANK)),
            proj_w(FOURIER_W), proj_w(CONV_W), proj_w(MLA_W),
            gate_w(0), gate_w(1), gate_w(2),
            gate_b(0), gate_b(1), gate_b(2),
        ],
        out_specs=pl.BlockSpec((tm, tn), lambda i, j: (i, j)),
        out_shape=jax.ShapeDtypeStruct((n_rows, D_MODEL), BF16),
        compiler_params=_cparams(("parallel", "parallel")),
        name="merge",
    )(*latent, *(ctx or ()), u, w_pf, w_pc, w_pm, w_gb, w_gb, w_gb, b_g, b_g, b_g)


_ROPE_SWAP = tuple(
    ax * 2 * ROPE_FREQS + (1 - half) * ROPE_FREQS + f
    for ax in range(2) for half in range(2) for f in range(ROPE_FREQS))


def _rope_table(tm):
    n_rows = SEQ // GRID_W
    rows = jnp.repeat(jnp.arange(n_rows, dtype=F32), GRID_W)
    cols = jnp.tile(jnp.arange(GRID_W, dtype=F32), n_rows)
    pos = jnp.stack([rows, cols], axis=-1)
    inv_freq = jnp.power(ROPE_THETA, -jnp.arange(ROPE_FREQS, dtype=F32) / ROPE_FREQS)
    ang = pos[:, :, None] * inv_freq
    cos, sin = jnp.cos(ang), jnp.sin(ang)
    cos64 = jnp.stack([cos, cos], axis=2).reshape(SEQ, QK_ROPE)
    sin64 = jnp.stack([-sin, sin], axis=2).reshape(SEQ, QK_ROPE)
    latent = jnp.concatenate([cos64, sin64], axis=-1)
    ident = jnp.concatenate([jnp.ones((tm, QK_ROPE), F32), jnp.zeros((tm, QK_ROPE), F32)], axis=-1)
    return jnp.concatenate([latent, ident], axis=0)


def _prep_weights(w_in, w_gate_a, w_uq, w_ukv):
    swap = jnp.array(_ROPE_SWAP, dtype=jnp.int32)
    kr = w_in[:, :, OFF_KV + KV_LORA:OFF_KV + KV_LORA + QK_ROPE]
    w_u = jnp.concatenate([
        w_in[:, :, :OFF_KV].astype(BF16),
        w_gate_a.astype(BF16),
        w_in[:, :, OFF_KV:OFF_KV + KV_LORA].astype(BF16),
        kr.astype(BF16),
        kr[:, :, swap].astype(BF16),
        jnp.zeros((DEPTH, D_MODEL, U_W - U_KR - 2 * QK_ROPE), BF16),
    ], axis=2)
    wq = w_uq.reshape(DEPTH, Q_LORA, MLA_HEADS, QK_NOPE + QK_ROPE)
    qr = wq[..., QK_NOPE:]
    w_q = jnp.concatenate([wq[..., :QK_NOPE], qr, qr[..., swap]], axis=-1)
    w_q = w_q.reshape(DEPTH, Q_LORA, MLA_HEADS * HEAD_W).astype(BF16)
    wkv = w_ukv.reshape(DEPTH, KV_LORA, MLA_HEADS, QK_NOPE + V_DIM)
    w_kv = jnp.concatenate([
        wkv[..., :QK_NOPE].reshape(DEPTH, KV_LORA, MLA_HEADS * QK_NOPE),
        wkv[..., QK_NOPE:].reshape(DEPTH, KV_LORA, MLA_W),
    ], axis=2).astype(BF16)
    return w_u, w_q, w_kv


def kernel(x, c, ctx, c_ctx, g_mix_pre, g_mix_post, g_mlp_pre, g_mlp_post, w_mod_a, w_mod_b, b_mod, w_in, conv_w, conv_b, conv_ln_g, conv_ln_b, q_norm_g, w_uq, kv_norm_g, w_ukv, w_pf, w_pc, w_pm, w_gate_a, w_gate_b, b_gate, w_out, w_ff1, w_ff2):
    rope = _rope_table(PROJ_TM)

    cc, sc = _dft_tables(FOURIER_GROUP_W)
    wc = (cc * (1.0 / 16.0)).astype(BF16)
    ws = (sc * (1.0 / 16.0)).astype(BF16)
    clx, slx = _dft_tables(SEQ)
    clx, slx = clx.astype(BF16), (-slx).astype(BF16)
    clc, slc = _dft_tables(CTX_LEN)
    clc, slc = clc.astype(BF16), (-slc).astype(BF16)

    def vec(a):
        return a[:, None, :]

    g_mix_pre, g_mix_post, g_mlp_pre, g_mlp_post = map(vec, (g_mix_pre, g_mix_post, g_mlp_pre, g_mlp_post))
    conv_b, conv_ln_g, conv_ln_b, q_norm_g, kv_norm_g, b_gate = map(
        vec, (conv_b, conv_ln_g, conv_ln_b, q_norm_g, kv_norm_g, b_gate))
    w_u, w_q, w_kv = _prep_weights(w_in, w_gate_a, w_uq, w_ukv)
    w_pf, w_pc, w_pm, w_gate_b, w_out, w_ff1, w_ff2 = (
        w.astype(BF16) for w in (w_pf, w_pc, w_pm, w_gate_b, w_out, w_ff1, w_ff2))

    cond = jnp.concatenate(
        [c, c_ctx[None, :], jnp.zeros((N_GROUPS - BATCH - 1, D_MODEL), F32)], axis=0)
    mod = _adaln(cond, w_mod_a.astype(BF16), w_mod_b.astype(BF16), vec(b_mod))
    mod = mod.reshape(DEPTH, N_GROUPS, N_MOD, D_MODEL).transpose(0, 2, 1, 3)[:, :, :, None, :]

    t = jnp.concatenate([x.reshape(ROWS_X, D_MODEL), ctx.reshape(ROWS_C, D_MODEL)], axis=0)
    h = _modulate(t, g_mix_pre, mod, 0, 0, 1)

    for i in range(DEPTH):
        last = i == DEPTH - 1
        n_rows = ROWS_X if last else ROWS

        u = _matmul(h, w_u, i, n_rows=ROWS, tm=1024, tn=768, tk=D_MODEL, name="in_proj")
        conv_args = (conv_w, conv_b, conv_ln_g, conv_ln_b, i)
        ab = _channel_dft(u, wc, ws, n_rows)
        q = _q_proj(u, q_norm_g, rope, w_q, i, n_rows)
        k, v = _kv_proj(u, kv_norm_g, rope, w_kv, i)
        latent = (_seq_dft(ab, clx, slx, seq=SEQ, row0=0),
                  _conformer_conv(u, *conv_args, seq=SEQ, row0=0),
                  _attn_latent(q, k, v))
        context = None if last else (
            _seq_dft(ab, clc, slc, seq=CTX_LEN, row0=ROWS_X),
            _conformer_conv(u, *conv_args, seq=CTX_LEN, row0=ROWS_X),
            _attn_ctx(q, k, v))
        y = _merge(latent, context, u, w_pf, w_pc, w_pm, w_gate_b, b_gate, i)
        yo = _matmul(y, w_out, i, n_rows=n_rows, tm=1024, tn=1024, tk=D_MODEL, name="out_proj")
        t, h = _residual_modulate(t, yo, g_mix_post, mod, i, 2, g_mlp_pre, i, 3, 4, n_rows)

        hid = _matmul(h, w_ff1, i, n_rows=n_rows, tm=1024, tn=1024, tk=D_MODEL, sq_relu=True, name="ff1")
        yo = _matmul(hid, w_ff2, i, n_rows=n_rows, tm=1024, tn=1024, tk=4096, name="ff2")
        if last:
            t = _residual(t, yo, g_mlp_post, mod, i, 5, n_rows)
        else:
            t, h = _residual_modulate(t, yo, g_mlp_post, mod, i, 5, g_mix_pre, i + 1, 0, 1, n_rows)

    return t.reshape(BATCH, SEQ, D_MODEL)
```

```python
import functools
import math

import jax
import jax.numpy as jnp
from jax import lax
from jax.experimental import pallas as pl
from jax.experimental.pallas import tpu as pltpu

F32 = jnp.float32
BF16 = jnp.bfloat16

D_MODEL = 4096
BATCH = 4
SEQ = 2048
DEPTH = 4
GRID_W = 64
CTX_LEN = 256
FOURIER_GROUPS = 4
FOURIER_GROUP_W = D_MODEL // 16
FOURIER_W = FOURIER_GROUPS * FOURIER_GROUP_W
CONV_W = D_MODEL // 4
CONV_K = 31
MLA_HEADS = D_MODEL // 256
Q_LORA = D_MODEL // 4
KV_LORA = D_MODEL // 8
QK_NOPE = 128
QK_ROPE = 64
V_DIM = 128
MLA_W = MLA_HEADS * V_DIM
ROPE_THETA = 10000.0
ROPE_FREQS = QK_ROPE // 4
ATTN_SCALE = (QK_NOPE + QK_ROPE) ** -0.5
Q_SCALE = ATTN_SCALE * math.log2(math.e)
N_BRANCH = 3
GATE_RANK = 512
MOD_RANK = 256
N_MOD = 6
D_FF = 4 * D_MODEL
EPS = 1e-6
OFF_C = FOURIER_W
OFF_Q = OFF_C + 2 * CONV_W
OFF_KV = OFF_Q + Q_LORA

ROWS_X = BATCH * SEQ
ROWS_C = BATCH * CTX_LEN
ROWS = ROWS_X + ROWS_C
N_GROUPS = 8
CTX_GROUP = BATCH
HEAD_W = 2 * QK_NOPE
CONV_PAD = 16
LANES = 128

U_F = 0
U_A = U_F + FOURIER_W
U_G = U_A + CONV_W
U_Q = U_G + CONV_W
T_HG = 0
T_CKV = T_HG + GATE_RANK
T_KR = T_CKV + KV_LORA
T_W = 1280

VMEM_LIMIT = 56 * 1024 * 1024


def _cparams(sem, limit=VMEM_LIMIT):
    return pltpu.CompilerParams(dimension_semantics=sem, vmem_limit_bytes=limit)


def _group_of_tile(i, tm):
    return jnp.where(i < ROWS_X // tm, i // (SEQ // tm), CTX_GROUP)


def _rope_block_of_tile(i, tm):
    return jnp.where(i < ROWS_X // tm, i % (SEQ // tm), SEQ // tm)


def _rms(x, g):
    return x * lax.rsqrt(jnp.mean(x * x, axis=-1, keepdims=True) + EPS) * g


def _mod_spec(layer, k, tm):
    return pl.BlockSpec((None, None, None, 1, D_MODEL),
                        lambda i: (layer, k, _group_of_tile(i, tm), 0, 0))


def _gain_spec(layer, width=D_MODEL):
    return pl.BlockSpec((None, 1, width), lambda *_: (layer, 0, 0))


def _adaln_kernel(c_ref, wa_ref, wb_ref, b_ref, o_ref):
    c = c_ref[...]
    s = (c * jax.nn.sigmoid(c)).astype(BF16)
    t = jnp.dot(s, wa_ref[...], preferred_element_type=F32)
    o_ref[...] = jnp.dot(t.astype(BF16), wb_ref[...], preferred_element_type=F32) + b_ref[...]


def _adaln(cond, w_a, w_b, b):
    tn = D_MODEL
    n = N_MOD * D_MODEL
    return pl.pallas_call(
        _adaln_kernel,
        grid=(DEPTH, n // tn),
        in_specs=[
            pl.BlockSpec((N_GROUPS, D_MODEL), lambda l, j: (0, 0)),
            pl.BlockSpec((None, D_MODEL, MOD_RANK), lambda l, j: (l, 0, 0)),
            pl.BlockSpec((None, MOD_RANK, tn), lambda l, j: (l, 0, j)),
            pl.BlockSpec((None, 1, tn), lambda l, j: (l, 0, j)),
        ],
        out_specs=pl.BlockSpec((None, N_GROUPS, tn), lambda l, j: (l, 0, j)),
        out_shape=jax.ShapeDtypeStruct((DEPTH, N_GROUPS, n), F32),
        compiler_params=_cparams(("parallel", "parallel")),
        name="adaln",
    )(cond, w_a, w_b, b)


ROW_TM = 256


def _modulated(x, g_ref, sh_ref, sc_ref):
    return (_rms(x, g_ref[...]) * (1.0 + sc_ref[...]) + sh_ref[...]).astype(BF16)


def _modulate_kernel(x_ref, g_ref, sh_ref, sc_ref, o_ref):
    o_ref[...] = _modulated(x_ref[...], g_ref, sh_ref, sc_ref)


def _modulate(x, g, mod, layer, k_shift, k_scale):
    tm = ROW_TM
    row = pl.BlockSpec((tm, D_MODEL), lambda i: (i, 0))
    return pl.pallas_call(
        _modulate_kernel,
        grid=(ROWS // tm,),
        in_specs=[row, _gain_spec(layer), _mod_spec(layer, k_shift, tm), _mod_spec(layer, k_scale, tm)],
        out_specs=row,
        out_shape=jax.ShapeDtypeStruct((ROWS, D_MODEL), BF16),
        compiler_params=_cparams(("parallel",)),
        name="modulate",
    )(x, g, mod, mod)


def _residual_kernel(x_ref, y_ref, g_ref, gate_ref, o_ref):
    o_ref[...] = x_ref[...] + gate_ref[...] * _rms(y_ref[...].astype(F32), g_ref[...])


def _residual(x, y, g, mod, layer, k_gate, n_rows):
    tm = ROW_TM
    row = pl.BlockSpec((tm, D_MODEL), lambda i: (i, 0))
    return pl.pallas_call(
        _residual_kernel,
        grid=(n_rows // tm,),
        in_specs=[row, row, _gain_spec(layer), _mod_spec(layer, k_gate, tm)],
        out_specs=row,
        out_shape=jax.ShapeDtypeStruct((n_rows, D_MODEL), F32),
        compiler_params=_cparams(("parallel",)),
        name="residual",
    )(x, y, g, mod)


def _resmod_kernel(x_ref, y_ref, gpost_ref, gate_ref, gpre_ref, sh_ref, sc_ref, xo_ref, h_ref):
    xn = x_ref[...] + gate_ref[...] * _rms(y_ref[...].astype(F32), gpost_ref[...])
    xo_ref[...] = xn
    h_ref[...] = _modulated(xn, gpre_ref, sh_ref, sc_ref)


def _residual_modulate(x, y, g_post, mod, layer, k_gate, g_pre, layer_pre, k_shift, k_scale, n_rows):
    tm = ROW_TM
    row = pl.BlockSpec((tm, D_MODEL), lambda i: (i, 0))
    return pl.pallas_call(
        _resmod_kernel,
        grid=(n_rows // tm,),
        in_specs=[row, row, _gain_spec(layer), _mod_spec(layer, k_gate, tm),
                  _gain_spec(layer_pre), _mod_spec(layer_pre, k_shift, tm),
                  _mod_spec(layer_pre, k_scale, tm)],
        out_specs=[row, row],
        out_shape=[jax.ShapeDtypeStruct((n_rows, D_MODEL), F32),
                   jax.ShapeDtypeStruct((n_rows, D_MODEL), BF16)],
        compiler_params=_cparams(("parallel",)),
        name="residual_modulate",
    )(x, y, g_post, mod, g_pre, mod, mod)


def _mm_kernel(*refs, nk, sq_relu, with_cast):
    if with_cast:
        a_ref, w_ref, src_ref, o_ref, cast_ref, *scratch = refs
        cast_ref[...] = src_ref[...].astype(BF16)
    else:
        a_ref, w_ref, o_ref, *scratch = refs

    def finish(r):
        if sq_relu:
            r = jnp.square(jnp.maximum(r, 0.0))
        o_ref[...] = r.astype(o_ref.dtype)

    p = jnp.dot(a_ref[...], w_ref[...], preferred_element_type=F32)
    if nk == 1:
        finish(p)
        return
    acc_ref, = scratch
    k = pl.program_id(2)

    @pl.when(k == 0)
    def _():
        acc_ref[...] = p

    @pl.when(k > 0)
    def _():
        acc_ref[...] += p

    @pl.when(k == nk - 1)
    def _():
        finish(acc_ref[...])


def _matmul(a, w, layer=None, *, n_rows, tm, tn, tk, out_dtype=BF16, sq_relu=False, name, cast=None):
    kdim, n = w.shape[-2:]
    nj, nk = n // tn, kdim // tk
    if w.ndim == 3:
        w_spec = pl.BlockSpec((None, tk, tn), lambda i, j, k: (layer, k, j))
    else:
        w_spec = pl.BlockSpec((tk, tn), lambda i, j, k: (k, j))
    in_specs = [pl.BlockSpec((tm, tk), lambda i, j, k: (i, k)), w_spec]
    out_specs = [pl.BlockSpec((tm, tn), lambda i, j, k: (i, j))]
    out_shape = [jax.ShapeDtypeStruct((n_rows, n), out_dtype)]
    operands = [a, w]
    if cast is not None:
        src, src_layer, n_cols, n_tiles = cast
        assert n_tiles <= (n_rows // tm) * nj * nk
        src_rows = src.shape[1]
        tile_rows = src_rows // n_tiles

        def tile(i, j, k):
            return jnp.minimum((i * nj + j) * nk + k, n_tiles - 1)

        in_specs.append(pl.BlockSpec((None, tile_rows, n_cols), lambda i, j, k: (src_layer, tile(i, j, k), 0)))
        out_specs.append(pl.BlockSpec((tile_rows, n_cols), lambda i, j, k: (tile(i, j, k), 0)))
        out_shape.append(jax.ShapeDtypeStruct((src_rows, n_cols), BF16))
        operands.append(src)
    res = pl.pallas_call(
        functools.partial(_mm_kernel, nk=nk, sq_relu=sq_relu, with_cast=cast is not None),
        grid=(n_rows // tm, nj, nk),
        in_specs=in_specs,
        out_specs=out_specs,
        out_shape=out_shape,
        scratch_shapes=[pltpu.VMEM((tm, tn), F32)] if nk > 1 else [],
        compiler_params=_cparams(("arbitrary", "arbitrary", "arbitrary")),
        name=name,
    )(*operands)
    return res if cast is not None else res[0]


def _chdft_kernel(u_ref, wc_ref, ws_ref, ab_ref):
    gw = FOURIER_GROUP_W
    for g in range(FOURIER_GROUPS):
        ug = u_ref[:, g * gw:(g + 1) * gw]
        ab_ref[:, g * gw:(g + 1) * gw] = jnp.dot(
            ug, wc_ref[...], preferred_element_type=F32).astype(BF16)
        ab_ref[:, FOURIER_W + g * gw:FOURIER_W + (g + 1) * gw] = jnp.dot(
            ug, ws_ref[...], preferred_element_type=F32).astype(BF16)


def _channel_dft(u, wc, ws, n_rows):
    tm = 1024
    gw = FOURIER_GROUP_W
    return pl.pallas_call(
        _chdft_kernel,
        grid=(n_rows // tm,),
        in_specs=[
            pl.BlockSpec((tm, FOURIER_W), lambda i: (i, U_F // FOURIER_W)),
            pl.BlockSpec((gw, gw), lambda i: (0, 0)),
            pl.BlockSpec((gw, gw), lambda i: (0, 0)),
        ],
        out_specs=pl.BlockSpec((tm, 2 * FOURIER_W), lambda i: (i, 0)),
        out_shape=jax.ShapeDtypeStruct((n_rows, 2 * FOURIER_W), BF16),
        compiler_params=_cparams(("parallel",)),
        name="channel_dft",
    )(u, wc, ws)


def _seqdft_kernel(cl_ref, sl_ref, a_ref, b_ref, o_ref, *, scale):
    r = jnp.dot(cl_ref[...], a_ref[...], preferred_element_type=F32)
    r = r + jnp.dot(sl_ref[...], b_ref[...], preferred_element_type=F32)
    o_ref[...] = (r * scale).astype(o_ref.dtype)


def _seq_dft(ab, cl, sl, *, seq, row0):
    tm = min(seq, 1024)
    nm = seq // tm
    blk0 = row0 // seq
    return pl.pallas_call(
        functools.partial(_seqdft_kernel, scale=float(seq) ** -0.5),
        grid=(nm, BATCH),
        in_specs=[
            pl.BlockSpec((tm, seq), lambda m, b: (m, 0)),
            pl.BlockSpec((tm, seq), lambda m, b: (m, 0)),
            pl.BlockSpec((seq, FOURIER_W), lambda m, b: (blk0 + b, 0)),
            pl.BlockSpec((seq, FOURIER_W), lambda m, b: (blk0 + b, 1)),
        ],
        out_specs=pl.BlockSpec((tm, FOURIER_W), lambda m, b: (b * nm + m, 0)),
        out_shape=jax.ShapeDtypeStruct((BATCH * seq, FOURIER_W), BF16),
        compiler_params=_cparams(("parallel", "parallel")),
        name="seq_dft_%d" % seq,
    )(cl, sl, ab, ab)


def _dft_tables(n):
    idx = jnp.arange(n, dtype=jnp.int32)
    ang = ((idx[:, None] * idx[None, :]) % n).astype(F32) * (2.0 * jnp.pi / n)
    return jnp.cos(ang), jnp.sin(ang)


CONV_ROWS = 64


def _conv_kernel(a_ref, gt_ref, w_ref, b_ref, lg_ref, lb_ref, o_ref, vpad_ref, y_ref, *, seq):
    zeros = jnp.zeros((CONV_PAD, CONV_W), F32)
    vpad_ref[0:CONV_PAD, :] = zeros
    vpad_ref[CONV_PAD + seq:2 * CONV_PAD + seq, :] = zeros
    glu_rows = 256

    def glu(i, carry):
        r0 = pl.multiple_of(i * glu_rows, glu_rows)
        a = a_ref[pl.ds(r0, glu_rows), :].astype(F32)
        g = gt_ref[pl.ds(r0, glu_rows), :].astype(F32)
        vpad_ref[pl.ds(r0 + CONV_PAD, glu_rows), :] = a * jax.nn.sigmoid(g)
        return carry

    lax.fori_loop(0, seq // glu_rows, glu, 0)

    win_rows = CONV_ROWS + 2 * CONV_PAD

    def step(i, carry):
        r0 = pl.multiple_of(i * CONV_ROWS, CONV_ROWS)
        for c in range(CONV_W // LANES):
            cols = slice(c * LANES, (c + 1) * LANES)
            win = vpad_ref[pl.ds(r0, win_rows), cols]
            acc = jnp.zeros((CONV_ROWS, LANES), F32)
            for b in range(8):
                wb = win if b == 0 else pltpu.roll(win, win_rows - b, axis=0)
                for a in range(4):
                    j = 8 * a + b
                    if j == 0:
                        continue
                    acc = acc + wb[8 * a:8 * a + CONV_ROWS, :] * w_ref[j - 1:j, cols]
            y_ref[:, cols] = acc + b_ref[:, cols]
        y = y_ref[...]
        yc = y - jnp.mean(y, axis=-1, keepdims=True)
        n = yc * lax.rsqrt(jnp.mean(yc * yc, axis=-1, keepdims=True) + EPS)
        n = n * lg_ref[...] + lb_ref[...]
        o_ref[pl.ds(r0, CONV_ROWS), :] = (n * jax.nn.sigmoid(n)).astype(o_ref.dtype)
        return carry

    lax.fori_loop(0, seq // CONV_ROWS, step, 0)


def _conformer_conv(u, conv_w, conv_b, ln_g, ln_b, layer, *, seq, row0):
    blk0 = row0 // seq
    vec = _gain_spec(layer, CONV_W)
    return pl.pallas_call(
        functools.partial(_conv_kernel, seq=seq),
        grid=(BATCH,),
        in_specs=[
            pl.BlockSpec((seq, CONV_W), lambda b: (blk0 + b, U_A // CONV_W)),
            pl.BlockSpec((seq, CONV_W), lambda b: (blk0 + b, U_G // CONV_W)),
            pl.BlockSpec((None, CONV_K, CONV_W), lambda b: (layer, 0, 0)),
            vec, vec, vec,
        ],
        out_specs=pl.BlockSpec((seq, CONV_W), lambda b: (b, 0)),
        out_shape=jax.ShapeDtypeStruct((BATCH * seq, CONV_W), BF16),
        scratch_shapes=[
            pltpu.VMEM((seq + 2 * CONV_PAD, CONV_W), F32),
            pltpu.VMEM((CONV_ROWS, CONV_W), F32),
        ],
        compiler_params=_cparams(("parallel",)),
        name="conformer_conv_%d" % seq,
    )(u, u, conv_w, conv_b, ln_g, ln_b)


PROJ_TM = 512


def _rotate(blk, table):
    t = blk * table
    return t + pltpu.roll(t, QK_ROPE, axis=1)


def _qproj_kernel(cq_ref, g_ref, t_ref, w_ref, q_ref, cn_ref, *, heads_per_step):
    @pl.when(pl.program_id(1) == 0)
    def _():
        cn_ref[...] = _rms(cq_ref[...].astype(F32), g_ref[...]).astype(BF16)

    acc = jnp.dot(cn_ref[...], w_ref[...], preferred_element_type=F32) * Q_SCALE
    table = t_ref[...]
    for h in range(heads_per_step):
        lo = h * HEAD_W
        q_ref[:, lo:lo + QK_NOPE] = acc[:, lo:lo + QK_NOPE].astype(BF16)
        q_ref[:, lo + QK_NOPE:lo + HEAD_W] = _rotate(
            acc[:, lo + QK_NOPE:lo + HEAD_W], table).astype(BF16)


def _q_proj(u, g, rope, w, layer, n_rows):
    tm = PROJ_TM
    hps = 4
    tn = hps * HEAD_W
    return pl.pallas_call(
        functools.partial(_qproj_kernel, heads_per_step=hps),
        grid=(n_rows // tm, MLA_HEADS // hps),
        in_specs=[
            pl.BlockSpec((tm, Q_LORA), lambda i, j: (i, U_Q // Q_LORA)),
            _gain_spec(layer, Q_LORA),
            pl.BlockSpec((tm, LANES), lambda i, j: (_rope_block_of_tile(i, tm), 0)),
            pl.BlockSpec((None, Q_LORA, tn), lambda i, j: (layer, 0, j)),
        ],
        out_specs=pl.BlockSpec((tm, tn), lambda i, j: (i, j)),
        out_shape=jax.ShapeDtypeStruct((n_rows, MLA_HEADS * HEAD_W), BF16),
        scratch_shapes=[pltpu.VMEM((tm, Q_LORA), BF16)],
        compiler_params=_cparams(("parallel", "arbitrary")),
        name="q_proj",
    )(u, g, rope, w)


def _kvproj_kernel(ckv_ref, kr_ref, t_ref, g_ref, wk_ref, wv_ref, k_ref, v_ref):
    cn = _rms(ckv_ref[...].astype(F32), g_ref[...]).astype(BF16)
    kn = jnp.dot(cn, wk_ref[...], preferred_element_type=F32)
    r = _rotate(kr_ref[...].astype(F32), t_ref[...])
    lane = lax.broadcasted_iota(jnp.int32, r.shape, 1)
    kr = jnp.where(lane < QK_ROPE, r, 0.0).astype(BF16)
    for h in range(MLA_HEADS):
        k_ref[:, h * HEAD_W:h * HEAD_W + QK_NOPE] = kn[:, h * QK_NOPE:(h + 1) * QK_NOPE].astype(BF16)
        k_ref[:, h * HEAD_W + QK_NOPE:(h + 1) * HEAD_W] = kr
    v = jnp.dot(cn, wv_ref[...], preferred_element_type=F32)
    ones = jnp.ones((v.shape[0], V_DIM), BF16)
    for h in range(MLA_HEADS):
        v_ref[:, h * HEAD_W:h * HEAD_W + V_DIM] = v[:, h * V_DIM:(h + 1) * V_DIM].astype(BF16)
        v_ref[:, h * HEAD_W + V_DIM:(h + 1) * HEAD_W] = ones


def _kv_proj(u, g, rope, w, layer):
    tm = PROJ_TM
    nk = MLA_HEADS * QK_NOPE
    slot = pl.BlockSpec((tm, MLA_HEADS * HEAD_W), lambda i: (i, 0))
    shape = jax.ShapeDtypeStruct((ROWS, MLA_HEADS * HEAD_W), BF16)
    return pl.pallas_call(
        _kvproj_kernel,
        grid=(ROWS // tm,),
        in_specs=[
            pl.BlockSpec((tm, KV_LORA), lambda i: (i, T_CKV // KV_LORA)),
            pl.BlockSpec((tm, LANES), lambda i: (i, T_KR // LANES)),
            pl.BlockSpec((tm, LANES), lambda i: (_rope_block_of_tile(i, tm), 0)),
            _gain_spec(layer, KV_LORA),
            pl.BlockSpec((None, KV_LORA, nk), lambda i: (layer, 0, 0)),
            pl.BlockSpec((None, KV_LORA, MLA_W), lambda i: (layer, 0, nk // MLA_W)),
        ],
        out_specs=[slot, slot],
        out_shape=[shape, shape],
        compiler_params=_cparams(("parallel",)),
        name="kv_proj",
    )(u, u, rope, g, w, w)


_NT = (((1,), (1,)), ((), ()))
ATTN_TQ = 1024
ATTN_CHUNK = 256


def _softmax_pv(s_blocks, v_refs):
    m = functools.reduce(jnp.maximum, [jnp.max(s, axis=-1, keepdims=True) for s in s_blocks])
    o = None
    for s, v_ref in zip(s_blocks, v_refs):
        pv = jnp.dot(jnp.exp2(s - m).astype(BF16), v_ref[...], preferred_element_type=F32)
        o = pv if o is None else o + pv
    return o[:, :V_DIM] / o[:, V_DIM:]


def _attn_latent_kernel(q_ref, kx_ref, kc_ref, vx_ref, vc_ref, o_ref):
    for c in range(ATTN_TQ // ATTN_CHUNK):
        rows = slice(c * ATTN_CHUNK, (c + 1) * ATTN_CHUNK)
        q = q_ref[rows, :]
        s1 = lax.dot_general(q, kx_ref[...], _NT, preferred_element_type=F32)
        s2 = lax.dot_general(q, kc_ref[...], _NT, preferred_element_type=F32)
        o_ref[rows, :] = _softmax_pv([s1, s2], [vx_ref, vc_ref]).astype(o_ref.dtype)


def _attn_latent(q, k, v):
    tq = ATTN_TQ
    nq = SEQ // tq
    cblk = ROWS_X // CTX_LEN
    return pl.pallas_call(
        _attn_latent_kernel,
        grid=(BATCH, MLA_HEADS, nq),
        in_specs=[
            pl.BlockSpec((tq, HEAD_W), lambda b, h, i: (b * nq + i, h)),
            pl.BlockSpec((SEQ, HEAD_W), lambda b, h, i: (b, h)),
            pl.BlockSpec((CTX_LEN, HEAD_W), lambda b, h, i: (cblk + b, h)),
            pl.BlockSpec((SEQ, HEAD_W), lambda b, h, i: (b, h)),
            pl.BlockSpec((CTX_LEN, HEAD_W), lambda b, h, i: (cblk + b, h)),
        ],
        out_specs=pl.BlockSpec((tq, V_DIM), lambda b, h, i: (b * nq + i, h)),
        out_shape=jax.ShapeDtypeStruct((ROWS_X, MLA_W), BF16),
        compiler_params=_cparams(("parallel", "parallel", "arbitrary")),
        name="attn_latent",
    )(q, k, k, v, v)


def _attn_ctx_kernel(q_ref, k_ref, v_ref, o_ref):
    s = lax.dot_general(q_ref[...], k_ref[...], _NT, preferred_element_type=F32)
    o_ref[...] = _softmax_pv([s], [v_ref]).astype(o_ref.dtype)


def _attn_ctx(q, k, v):
    cblk = ROWS_X // CTX_LEN
    slot = pl.BlockSpec((CTX_LEN, HEAD_W), lambda b, h: (cblk + b, h))
    return pl.pallas_call(
        _attn_ctx_kernel,
        grid=(BATCH, MLA_HEADS),
        in_specs=[slot, slot, slot],
        out_specs=pl.BlockSpec((CTX_LEN, V_DIM), lambda b, h: (b, h)),
        out_shape=jax.ShapeDtypeStruct((ROWS_C, MLA_W), BF16),
        compiler_params=_cparams(("parallel", "parallel")),
        name="attn_ctx",
    )(q, k, v)


MERGE_TM = 512
MERGE_TN = 512


def _merge_kernel(*refs, with_ctx):
    n_act = 6 if with_ctx else 3
    acts = refs[:n_act]
    hg_ref, wf_ref, wc_ref, wm_ref, gf_ref, gc_ref, gm_ref, bf_ref, bc_ref, bm_ref, y_ref = refs[n_act:]
    if with_ctx:
        is_latent = pl.program_id(0) < ROWS_X // MERGE_TM
        f, c, m = (jnp.where(is_latent, acts[k][...], acts[k + 3][...]) for k in range(3))
    else:
        f, c, m = (r[...] for r in acts)
    hg = hg_ref[...]

    def gated(x, w_ref, wg_ref, bg_ref):
        gate = jax.nn.sigmoid(jnp.dot(hg, wg_ref[...], preferred_element_type=F32) + bg_ref[...])
        return gate * jnp.dot(x, w_ref[...], preferred_element_type=F32)

    y = gated(f, wf_ref, gf_ref, bf_ref) + gated(c, wc_ref, gc_ref, bc_ref) + gated(m, wm_ref, gm_ref, bm_ref)
    y_ref[...] = y.astype(y_ref.dtype)


def _merge(latent, ctx, u, w_pf, w_pc, w_pm, w_gb, b_g, layer):
    tm, tn = MERGE_TM, MERGE_TN
    nb = D_MODEL // tn
    nxt = ROWS_X // tm
    with_ctx = ctx is not None
    n_rows = ROWS if with_ctx else ROWS_X
    widths = (FOURIER_W, CONV_W, MLA_W)

    act_specs = [pl.BlockSpec((tm, w), lambda i, j: (jnp.minimum(i, nxt - 1), 0)) for w in widths]
    if with_ctx:
        act_specs += [pl.BlockSpec((tm, w), lambda i, j: (jnp.maximum(i - nxt, 0), 0)) for w in widths]

    def proj_w(w):
        return pl.BlockSpec((None, w, tn), lambda i, j: (layer, 0, j))

    def gate_w(k):
        return pl.BlockSpec((None, GATE_RANK, tn), lambda i, j: (layer, 0, k * nb + j))

    def gate_b(k):
        return pl.BlockSpec((None, 1, tn), lambda i, j: (layer, 0, k * nb + j))

    return pl.pallas_call(
        functools.partial(_merge_kernel, with_ctx=with_ctx),
        grid=(n_rows // tm, nb),
        in_specs=act_specs + [
            pl.BlockSpec((tm, GATE_RANK), lambda i, j: (i, T_HG // GATE_RANK)),
            proj_w(FOURIER_W), proj_w(CONV_W), proj_w(MLA_W),
            gate_w(0), gate_w(1), gate_w(2),
            gate_b(0), gate_b(1), gate_b(2),
        ],
        out_specs=pl.BlockSpec((tm, tn), lambda i, j: (i, j)),
        out_shape=jax.ShapeDtypeStruct((n_rows, D_MODEL), BF16),
        compiler_params=_cparams(("parallel", "parallel")),
        name="merge",
    )(*latent, *(ctx or ()), u, w_pf, w_pc, w_pm, w_gb, w_gb, w_gb, b_g, b_g, b_g)


_ROPE_SWAP = tuple(
    ax * 2 * ROPE_FREQS + (1 - half) * ROPE_FREQS + f
    for ax in range(2) for half in range(2) for f in range(ROPE_FREQS))


def _rope_table(tm):
    n_rows = SEQ // GRID_W
    rows = jnp.repeat(jnp.arange(n_rows, dtype=F32), GRID_W)
    cols = jnp.tile(jnp.arange(GRID_W, dtype=F32), n_rows)
    pos = jnp.stack([rows, cols], axis=-1)
    inv_freq = jnp.power(ROPE_THETA, -jnp.arange(ROPE_FREQS, dtype=F32) / ROPE_FREQS)
    ang = pos[:, :, None] * inv_freq
    cos, sin = jnp.cos(ang), jnp.sin(ang)
    cos64 = jnp.stack([cos, cos], axis=2).reshape(SEQ, QK_ROPE)
    sin64 = jnp.stack([-sin, sin], axis=2).reshape(SEQ, QK_ROPE)
    latent = jnp.concatenate([cos64, sin64], axis=-1)
    ident = jnp.concatenate([jnp.ones((tm, QK_ROPE), F32), jnp.zeros((tm, QK_ROPE), F32)], axis=-1)
    return jnp.concatenate([latent, ident], axis=0)


def _prep_weights(w_in, w_gate_a, w_uq, w_ukv):
    swap = jnp.array(_ROPE_SWAP, dtype=jnp.int32)
    kr = w_in[:, :, OFF_KV + KV_LORA:OFF_KV + KV_LORA + QK_ROPE]
    w_tail = jnp.concatenate([
        w_gate_a.astype(BF16),
        w_in[:, :, OFF_KV:OFF_KV + KV_LORA].astype(BF16),
        kr.astype(BF16),
        kr[:, :, swap].astype(BF16),
        jnp.zeros((DEPTH, D_MODEL, T_W - T_KR - 2 * QK_ROPE), BF16),
    ], axis=2)
    wq = w_uq.reshape(DEPTH, Q_LORA, MLA_HEADS, QK_NOPE + QK_ROPE)
    qr = wq[..., QK_NOPE:]
    w_q = jnp.concatenate([wq[..., :QK_NOPE], qr, qr[..., swap]], axis=-1)
    w_q = w_q.reshape(DEPTH, Q_LORA, MLA_HEADS * HEAD_W).astype(BF16)
    wkv = w_ukv.reshape(DEPTH, KV_LORA, MLA_HEADS, QK_NOPE + V_DIM)
    w_kv = jnp.concatenate([
        wkv[..., :QK_NOPE].reshape(DEPTH, KV_LORA, MLA_HEADS * QK_NOPE),
        wkv[..., QK_NOPE:].reshape(DEPTH, KV_LORA, MLA_W),
    ], axis=2).astype(BF16)
    return w_tail, w_q, w_kv


def kernel(x, c, ctx, c_ctx, g_mix_pre, g_mix_post, g_mlp_pre, g_mlp_post, w_mod_a, w_mod_b, b_mod, w_in, conv_w, conv_b, conv_ln_g, conv_ln_b, q_norm_g, w_uq, kv_norm_g, w_ukv, w_pf, w_pc, w_pm, w_gate_a, w_gate_b, b_gate, w_out, w_ff1, w_ff2):
    rope = _rope_table(PROJ_TM)

    cc, sc = _dft_tables(FOURIER_GROUP_W)
    wc = (cc * (1.0 / 16.0)).astype(BF16)
    ws = (sc * (1.0 / 16.0)).astype(BF16)
    clx, slx = _dft_tables(SEQ)
    clx, slx = clx.astype(BF16), (-slx).astype(BF16)
    clc, slc = _dft_tables(CTX_LEN)
    clc, slc = clc.astype(BF16), (-slc).astype(BF16)

    def vec(a):
        return a[:, None, :]

    g_mix_pre, g_mix_post, g_mlp_pre, g_mlp_post = map(vec, (g_mix_pre, g_mix_post, g_mlp_pre, g_mlp_post))
    conv_b, conv_ln_g, conv_ln_b, q_norm_g, kv_norm_g, b_gate = map(
        vec, (conv_b, conv_ln_g, conv_ln_b, q_norm_g, kv_norm_g, b_gate))
    w_tail, w_q, w_kv = _prep_weights(w_in, w_gate_a, w_uq, w_ukv)
    w_pf, w_pc, w_pm, w_gate_b = (w.astype(BF16) for w in (w_pf, w_pc, w_pm, w_gate_b))
    w_main = w_in[0, :, :OFF_KV].astype(BF16)
    w_up = w_ff1[0].astype(BF16)
    small_tiles, big_tiles = 32, 128

    cond = jnp.concatenate(
        [c, c_ctx[None, :], jnp.zeros((N_GROUPS - BATCH - 1, D_MODEL), F32)], axis=0)
    mod = _adaln(cond, w_mod_a.astype(BF16), w_mod_b.astype(BF16), vec(b_mod))
    mod = mod.reshape(DEPTH, N_GROUPS, N_MOD, D_MODEL).transpose(0, 2, 1, 3)[:, :, :, None, :]

    t = jnp.concatenate([x.reshape(ROWS_X, D_MODEL), ctx.reshape(ROWS_C, D_MODEL)], axis=0)
    h = _modulate(t, g_mix_pre, mod, 0, 0, 1)

    for i in range(DEPTH):
        last = i == DEPTH - 1
        n_rows = ROWS_X if last else ROWS

        big = dict(tm=1024, tn=1024, tk=D_MODEL)
        u, w_o = _matmul(h, w_main, n_rows=n_rows, name="in_proj", **big,
                         cast=(w_out, i, D_MODEL, small_tiles))
        u_tail = _matmul(h, w_tail, i, n_rows=ROWS, tm=1024, tn=T_W, tk=D_MODEL, name="in_proj_tail")
        conv_args = (conv_w, conv_b, conv_ln_g, conv_ln_b, i)
        ab = _channel_dft(u, wc, ws, n_rows)
        q = _q_proj(u, q_norm_g, rope, w_q, i, n_rows)
        k, v = _kv_proj(u_tail, kv_norm_g, rope, w_kv, i)
        latent = (_seq_dft(ab, clx, slx, seq=SEQ, row0=0),
                  _conformer_conv(u, *conv_args, seq=SEQ, row0=0),
                  _attn_latent(q, k, v))
        context = None if last else (
            _seq_dft(ab, clc, slc, seq=CTX_LEN, row0=ROWS_X),
            _conformer_conv(u, *conv_args, seq=CTX_LEN, row0=ROWS_X),
            _attn_ctx(q, k, v))
        y = _merge(latent, context, u_tail, w_pf, w_pc, w_pm, w_gate_b, b_gate, i)
        if last:
            yo = _matmul(y, w_o, n_rows=n_rows, name="out_proj", **big)
        else:
            yo, w_main = _matmul(y, w_o, n_rows=n_rows, name="out_proj", **big,
                                 cast=(w_in, i + 1, OFF_KV, small_tiles))
        t, h = _residual_modulate(t, yo, g_mix_post, mod, i, 2, g_mlp_pre, i, 3, 4, n_rows)

        hid, w_down = _matmul(h, w_up, n_rows=n_rows, sq_relu=True, name="ff1", **big,
                              cast=(w_ff2, i, D_MODEL, big_tiles))
        if last:
            yo = _matmul(hid, w_down, n_rows=n_rows, name="ff2", **big)
            t = _residual(t, yo, g_mlp_post, mod, i, 5, n_rows)
        else:
            yo, w_up = _matmul(hid, w_down, n_rows=n_rows, name="ff2", **big,
                               cast=(w_ff1, i + 1, D_FF, big_tiles))
            t, h = _residual_modulate(t, yo, g_mlp_post, mod, i, 5, g_mix_pre, i + 1, 0, 1, n_rows)

    return t.reshape(BATCH, SEQ, D_MODEL)
```

```python
import functools
import math

import jax
import jax.numpy as jnp
from jax import lax
from jax.experimental import pallas as pl
from jax.experimental.pallas import tpu as pltpu

F32 = jnp.float32
BF16 = jnp.bfloat16

D_MODEL = 4096
BATCH = 4
SEQ = 2048
DEPTH = 4
GRID_W = 64
CTX_LEN = 256
FOURIER_GROUPS = 4
FOURIER_GROUP_W = D_MODEL // 16
FOURIER_W = FOURIER_GROUPS * FOURIER_GROUP_W
CONV_W = D_MODEL // 4
CONV_K = 31
MLA_HEADS = D_MODEL // 256
Q_LORA = D_MODEL // 4
KV_LORA = D_MODEL // 8
QK_NOPE = 128
QK_ROPE = 64
V_DIM = 128
MLA_W = MLA_HEADS * V_DIM
ROPE_THETA = 10000.0
ROPE_FREQS = QK_ROPE // 4
ATTN_SCALE = (QK_NOPE + QK_ROPE) ** -0.5
Q_SCALE = ATTN_SCALE * math.log2(math.e)
N_BRANCH = 3
GATE_RANK = 512
MOD_RANK = 256
N_MOD = 6
D_FF = 4 * D_MODEL
EPS = 1e-6
OFF_C = FOURIER_W
OFF_Q = OFF_C + 2 * CONV_W
OFF_KV = OFF_Q + Q_LORA

ROWS_X = BATCH * SEQ
ROWS_C = BATCH * CTX_LEN
ROWS = ROWS_X + ROWS_C
N_GROUPS = 8
CTX_GROUP = BATCH
HEAD_W = 2 * QK_NOPE
CONV_PAD = 16
LANES = 128

U_F = 0
U_A = U_F + FOURIER_W
U_G = U_A + CONV_W
U_Q = U_G + CONV_W
T_HG = 0
T_CKV = T_HG + GATE_RANK
T_KR = T_CKV + KV_LORA
T_W = 1280

VMEM_LIMIT = 56 * 1024 * 1024


def _cparams(sem, limit=VMEM_LIMIT):
    return pltpu.CompilerParams(dimension_semantics=sem, vmem_limit_bytes=limit)


def _group_of_tile(i, tm):
    return jnp.where(i < ROWS_X // tm, i // (SEQ // tm), CTX_GROUP)


def _rope_block_of_tile(i, tm):
    return jnp.where(i < ROWS_X // tm, i % (SEQ // tm), SEQ // tm)


def _rms(x, g):
    return x * lax.rsqrt(jnp.mean(x * x, axis=-1, keepdims=True) + EPS) * g


def _mod_spec(layer, k, tm):
    return pl.BlockSpec((None, None, None, 1, D_MODEL),
                        lambda i: (layer, k, _group_of_tile(i, tm), 0, 0))


def _gain_spec(layer, width=D_MODEL):
    return pl.BlockSpec((None, 1, width), lambda *_: (layer, 0, 0))


def _adaln_kernel(c_ref, wa_ref, wb_ref, b_ref, o_ref):
    c = c_ref[...]
    s = (c * jax.nn.sigmoid(c)).astype(BF16)
    t = jnp.dot(s, wa_ref[...], preferred_element_type=F32)
    o_ref[...] = jnp.dot(t.astype(BF16), wb_ref[...], preferred_element_type=F32) + b_ref[...]


def _adaln(cond, w_a, w_b, b):
    tn = D_MODEL
    n = N_MOD * D_MODEL
    return pl.pallas_call(
        _adaln_kernel,
        grid=(DEPTH, n // tn),
        in_specs=[
            pl.BlockSpec((N_GROUPS, D_MODEL), lambda l, j: (0, 0)),
            pl.BlockSpec((None, D_MODEL, MOD_RANK), lambda l, j: (l, 0, 0)),
            pl.BlockSpec((None, MOD_RANK, tn), lambda l, j: (l, 0, j)),
            pl.BlockSpec((None, 1, tn), lambda l, j: (l, 0, j)),
        ],
        out_specs=pl.BlockSpec((None, N_GROUPS, tn), lambda l, j: (l, 0, j)),
        out_shape=jax.ShapeDtypeStruct((DEPTH, N_GROUPS, n), F32),
        compiler_params=_cparams(("parallel", "parallel")),
        name="adaln",
    )(cond, w_a, w_b, b)


ROW_TM = 256


def _modulated(x, g_ref, sh_ref, sc_ref):
    return (_rms(x, g_ref[...]) * (1.0 + sc_ref[...]) + sh_ref[...]).astype(BF16)


def _modulate_kernel(x_ref, g_ref, sh_ref, sc_ref, o_ref):
    o_ref[...] = _modulated(x_ref[...], g_ref, sh_ref, sc_ref)


def _modulate(x, g, mod, layer, k_shift, k_scale):
    tm = ROW_TM
    row = pl.BlockSpec((tm, D_MODEL), lambda i: (i, 0))
    return pl.pallas_call(
        _modulate_kernel,
        grid=(ROWS // tm,),
        in_specs=[row, _gain_spec(layer), _mod_spec(layer, k_shift, tm), _mod_spec(layer, k_scale, tm)],
        out_specs=row,
        out_shape=jax.ShapeDtypeStruct((ROWS, D_MODEL), BF16),
        compiler_params=_cparams(("parallel",)),
        name="modulate",
    )(x, g, mod, mod)


def _residual_kernel(x_ref, y_ref, g_ref, gate_ref, o_ref):
    o_ref[...] = x_ref[...] + gate_ref[...] * _rms(y_ref[...].astype(F32), g_ref[...])


def _residual(x, y, g, mod, layer, k_gate, n_rows):
    tm = ROW_TM
    row = pl.BlockSpec((tm, D_MODEL), lambda i: (i, 0))
    return pl.pallas_call(
        _residual_kernel,
        grid=(n_rows // tm,),
        in_specs=[row, row, _gain_spec(layer), _mod_spec(layer, k_gate, tm)],
        out_specs=row,
        out_shape=jax.ShapeDtypeStruct((n_rows, D_MODEL), F32),
        compiler_params=_cparams(("parallel",)),
        name="residual",
    )(x, y, g, mod)


def _resmod_kernel(x_ref, y_ref, gpost_ref, gate_ref, gpre_ref, sh_ref, sc_ref, xo_ref, h_ref):
    xn = x_ref[...] + gate_ref[...] * _rms(y_ref[...].astype(F32), gpost_ref[...])
    xo_ref[...] = xn
    h_ref[...] = _modulated(xn, gpre_ref, sh_ref, sc_ref)


def _residual_modulate(x, y, g_post, mod, layer, k_gate, g_pre, layer_pre, k_shift, k_scale, n_rows):
    tm = ROW_TM
    row = pl.BlockSpec((tm, D_MODEL), lambda i: (i, 0))
    return pl.pallas_call(
        _resmod_kernel,
        grid=(n_rows // tm,),
        in_specs=[row, row, _gain_spec(layer), _mod_spec(layer, k_gate, tm),
                  _gain_spec(layer_pre), _mod_spec(layer_pre, k_shift, tm),
                  _mod_spec(layer_pre, k_scale, tm)],
        out_specs=[row, row],
        out_shape=[jax.ShapeDtypeStruct((n_rows, D_MODEL), F32),
                   jax.ShapeDtypeStruct((n_rows, D_MODEL), BF16)],
        compiler_params=_cparams(("parallel",)),
        name="residual_modulate",
    )(x, y, g_post, mod, g_pre, mod, mod)


def _mm_kernel(*refs, nk, sq_relu, with_cast):
    if with_cast:
        a_ref, w_ref, src_ref, o_ref, cast_ref, *scratch = refs
        cast_ref[...] = src_ref[...].astype(BF16)
    else:
        a_ref, w_ref, o_ref, *scratch = refs

    def finish(r):
        if sq_relu:
            r = jnp.square(jnp.maximum(r, 0.0))
        o_ref[...] = r.astype(o_ref.dtype)

    p = jnp.dot(a_ref[...], w_ref[...], preferred_element_type=F32)
    if nk == 1:
        finish(p)
        return
    acc_ref, = scratch
    k = pl.program_id(2)

    @pl.when(k == 0)
    def _():
        acc_ref[...] = p

    @pl.when(k > 0)
    def _():
        acc_ref[...] += p

    @pl.when(k == nk - 1)
    def _():
        finish(acc_ref[...])


def _matmul(a, w, layer=None, *, n_rows, tm, tn, tk, n_cols=None, out_dtype=BF16, sq_relu=False, name,
            cast=None):
    kdim = w.shape[-2]
    n = n_cols or w.shape[-1]
    nj, nk = n // tn, kdim // tk
    if w.ndim == 3:
        w_spec = pl.BlockSpec((None, tk, tn), lambda i, j, k: (layer, k, j))
    else:
        w_spec = pl.BlockSpec((tk, tn), lambda i, j, k: (k, j))
    in_specs = [pl.BlockSpec((tm, tk), lambda i, j, k: (i, k)), w_spec]
    out_specs = [pl.BlockSpec((tm, tn), lambda i, j, k: (i, j))]
    out_shape = [jax.ShapeDtypeStruct((n_rows, n), out_dtype)]
    operands = [a, w]
    if cast is not None:
        src, src_layer, n_cols, n_tiles = cast
        assert n_tiles <= (n_rows // tm) * nj * nk
        src_rows = src.shape[1]
        tile_rows = src_rows // n_tiles

        def tile(i, j, k):
            return jnp.minimum((i * nj + j) * nk + k, n_tiles - 1)

        in_specs.append(pl.BlockSpec((None, tile_rows, n_cols), lambda i, j, k: (src_layer, tile(i, j, k), 0)))
        out_specs.append(pl.BlockSpec((tile_rows, n_cols), lambda i, j, k: (tile(i, j, k), 0)))
        out_shape.append(jax.ShapeDtypeStruct((src_rows, n_cols), BF16))
        operands.append(src)
    res = pl.pallas_call(
        functools.partial(_mm_kernel, nk=nk, sq_relu=sq_relu, with_cast=cast is not None),
        grid=(n_rows // tm, nj, nk),
        in_specs=in_specs,
        out_specs=out_specs,
        out_shape=out_shape,
        scratch_shapes=[pltpu.VMEM((tm, tn), F32)] if nk > 1 else [],
        compiler_params=_cparams(("arbitrary", "arbitrary", "arbitrary")),
        name=name,
    )(*operands)
    return res if cast is not None else res[0]


def _chdft_kernel(u_ref, wc_ref, ws_ref, ab_ref):
    gw = FOURIER_GROUP_W
    for g in range(FOURIER_GROUPS):
        ug = u_ref[:, g * gw:(g + 1) * gw]
        ab_ref[:, g * gw:(g + 1) * gw] = jnp.dot(
            ug, wc_ref[...], preferred_element_type=F32).astype(BF16)
        ab_ref[:, FOURIER_W + g * gw:FOURIER_W + (g + 1) * gw] = jnp.dot(
            ug, ws_ref[...], preferred_element_type=F32).astype(BF16)


def _channel_dft(u, wc, ws, n_rows):
    tm = 1024
    gw = FOURIER_GROUP_W
    return pl.pallas_call(
        _chdft_kernel,
        grid=(n_rows // tm,),
        in_specs=[
            pl.BlockSpec((tm, FOURIER_W), lambda i: (i, U_F // FOURIER_W)),
            pl.BlockSpec((gw, gw), lambda i: (0, 0)),
            pl.BlockSpec((gw, gw), lambda i: (0, 0)),
        ],
        out_specs=pl.BlockSpec((tm, 2 * FOURIER_W), lambda i: (i, 0)),
        out_shape=jax.ShapeDtypeStruct((n_rows, 2 * FOURIER_W), BF16),
        compiler_params=_cparams(("parallel",)),
        name="channel_dft",
    )(u, wc, ws)


def _seqdft_kernel(cl_ref, sl_ref, a_ref, b_ref, o_ref, *, scale):
    r = jnp.dot(cl_ref[...], a_ref[...], preferred_element_type=F32)
    r = r + jnp.dot(sl_ref[...], b_ref[...], preferred_element_type=F32)
    o_ref[...] = (r * scale).astype(o_ref.dtype)


def _seq_dft(ab, cl, sl, *, seq, row0):
    tm = min(seq, 1024)
    nm = seq // tm
    blk0 = row0 // seq
    return pl.pallas_call(
        functools.partial(_seqdft_kernel, scale=float(seq) ** -0.5),
        grid=(nm, BATCH),
        in_specs=[
            pl.BlockSpec((tm, seq), lambda m, b: (m, 0)),
            pl.BlockSpec((tm, seq), lambda m, b: (m, 0)),
            pl.BlockSpec((seq, FOURIER_W), lambda m, b: (blk0 + b, 0)),
            pl.BlockSpec((seq, FOURIER_W), lambda m, b: (blk0 + b, 1)),
        ],
        out_specs=pl.BlockSpec((tm, FOURIER_W), lambda m, b: (b * nm + m, 0)),
        out_shape=jax.ShapeDtypeStruct((BATCH * seq, FOURIER_W), BF16),
        compiler_params=_cparams(("parallel", "parallel")),
        name="seq_dft_%d" % seq,
    )(cl, sl, ab, ab)


def _dft_tables(n):
    idx = jnp.arange(n, dtype=jnp.int32)
    ang = ((idx[:, None] * idx[None, :]) % n).astype(F32) * (2.0 * jnp.pi / n)
    return jnp.cos(ang), jnp.sin(ang)


CONV_ROWS = 64


def _conv_kernel(a_ref, gt_ref, w_ref, b_ref, lg_ref, lb_ref, o_ref, vpad_ref, y_ref, *, seq):
    zeros = jnp.zeros((CONV_PAD, CONV_W), F32)
    vpad_ref[0:CONV_PAD, :] = zeros
    vpad_ref[CONV_PAD + seq:2 * CONV_PAD + seq, :] = zeros
    glu_rows = 256

    def glu(i, carry):
        r0 = pl.multiple_of(i * glu_rows, glu_rows)
        a = a_ref[pl.ds(r0, glu_rows), :].astype(F32)
        g = gt_ref[pl.ds(r0, glu_rows), :].astype(F32)
        vpad_ref[pl.ds(r0 + CONV_PAD, glu_rows), :] = a * jax.nn.sigmoid(g)
        return carry

    lax.fori_loop(0, seq // glu_rows, glu, 0)

    win_rows = CONV_ROWS + 2 * CONV_PAD

    def step(i, carry):
        r0 = pl.multiple_of(i * CONV_ROWS, CONV_ROWS)
        for c in range(CONV_W // LANES):
            cols = slice(c * LANES, (c + 1) * LANES)
            win = vpad_ref[pl.ds(r0, win_rows), cols]
            acc = jnp.zeros((CONV_ROWS, LANES), F32)
            for b in range(8):
                wb = win if b == 0 else pltpu.roll(win, win_rows - b, axis=0)
                for a in range(4):
                    j = 8 * a + b
                    if j == 0:
                        continue
                    acc = acc + wb[8 * a:8 * a + CONV_ROWS, :] * w_ref[j - 1:j, cols]
            y_ref[:, cols] = acc + b_ref[:, cols]
        y = y_ref[...]
        yc = y - jnp.mean(y, axis=-1, keepdims=True)
        n = yc * lax.rsqrt(jnp.mean(yc * yc, axis=-1, keepdims=True) + EPS)
        n = n * lg_ref[...] + lb_ref[...]
        o_ref[pl.ds(r0, CONV_ROWS), :] = (n * jax.nn.sigmoid(n)).astype(o_ref.dtype)
        return carry

    lax.fori_loop(0, seq // CONV_ROWS, step, 0)


def _conformer_conv(u, conv_w, conv_b, ln_g, ln_b, layer, *, seq, row0):
    blk0 = row0 // seq
    vec = _gain_spec(layer, CONV_W)
    return pl.pallas_call(
        functools.partial(_conv_kernel, seq=seq),
        grid=(BATCH,),
        in_specs=[
            pl.BlockSpec((seq, CONV_W), lambda b: (blk0 + b, U_A // CONV_W)),
            pl.BlockSpec((seq, CONV_W), lambda b: (blk0 + b, U_G // CONV_W)),
            pl.BlockSpec((None, CONV_K, CONV_W), lambda b: (layer, 0, 0)),
            vec, vec, vec,
        ],
        out_specs=pl.BlockSpec((seq, CONV_W), lambda b: (b, 0)),
        out_shape=jax.ShapeDtypeStruct((BATCH * seq, CONV_W), BF16),
        scratch_shapes=[
            pltpu.VMEM((seq + 2 * CONV_PAD, CONV_W), F32),
            pltpu.VMEM((CONV_ROWS, CONV_W), F32),
        ],
        compiler_params=_cparams(("parallel",)),
        name="conformer_conv_%d" % seq,
    )(u, u, conv_w, conv_b, ln_g, ln_b)


PROJ_TM = 512


def _rotate(blk, table):
    t = blk * table
    return t + pltpu.roll(t, QK_ROPE, axis=1)


def _qproj_kernel(cq_ref, g_ref, t_ref, w_ref, q_ref, cn_ref, *, heads_per_step):
    @pl.when(pl.program_id(1) == 0)
    def _():
        cn_ref[...] = _rms(cq_ref[...].astype(F32), g_ref[...]).astype(BF16)

    acc = jnp.dot(cn_ref[...], w_ref[...], preferred_element_type=F32) * Q_SCALE
    table = t_ref[...]
    for h in range(heads_per_step):
        lo = h * HEAD_W
        q_ref[:, lo:lo + QK_NOPE] = acc[:, lo:lo + QK_NOPE].astype(BF16)
        q_ref[:, lo + QK_NOPE:lo + HEAD_W] = _rotate(
            acc[:, lo + QK_NOPE:lo + HEAD_W], table).astype(BF16)


def _q_proj(u, g, rope, w, layer, n_rows):
    tm = PROJ_TM
    hps = 4
    tn = hps * HEAD_W
    return pl.pallas_call(
        functools.partial(_qproj_kernel, heads_per_step=hps),
        grid=(n_rows // tm, MLA_HEADS // hps),
        in_specs=[
            pl.BlockSpec((tm, Q_LORA), lambda i, j: (i, U_Q // Q_LORA)),
            _gain_spec(layer, Q_LORA),
            pl.BlockSpec((tm, LANES), lambda i, j: (_rope_block_of_tile(i, tm), 0)),
            pl.BlockSpec((None, Q_LORA, tn), lambda i, j: (layer, 0, j)),
        ],
        out_specs=pl.BlockSpec((tm, tn), lambda i, j: (i, j)),
        out_shape=jax.ShapeDtypeStruct((n_rows, MLA_HEADS * HEAD_W), BF16),
        scratch_shapes=[pltpu.VMEM((tm, Q_LORA), BF16)],
        compiler_params=_cparams(("parallel", "arbitrary")),
        name="q_proj",
    )(u, g, rope, w)


def _kvproj_kernel(ckv_ref, kr_ref, t_ref, g_ref, wk_ref, wv_ref, k_ref, v_ref):
    cn = _rms(ckv_ref[...].astype(F32), g_ref[...]).astype(BF16)
    kn = jnp.dot(cn, wk_ref[...], preferred_element_type=F32)
    r = _rotate(kr_ref[...].astype(F32), t_ref[...])
    lane = lax.broadcasted_iota(jnp.int32, r.shape, 1)
    kr = jnp.where(lane < QK_ROPE, r, 0.0).astype(BF16)
    for h in range(MLA_HEADS):
        k_ref[:, h * HEAD_W:h * HEAD_W + QK_NOPE] = kn[:, h * QK_NOPE:(h + 1) * QK_NOPE].astype(BF16)
        k_ref[:, h * HEAD_W + QK_NOPE:(h + 1) * HEAD_W] = kr
    v = jnp.dot(cn, wv_ref[...], preferred_element_type=F32)
    ones = jnp.ones((v.shape[0], V_DIM), BF16)
    for h in range(MLA_HEADS):
        v_ref[:, h * HEAD_W:h * HEAD_W + V_DIM] = v[:, h * V_DIM:(h + 1) * V_DIM].astype(BF16)
        v_ref[:, h * HEAD_W + V_DIM:(h + 1) * HEAD_W] = ones


def _kv_proj(u, g, rope, w, layer):
    tm = PROJ_TM
    nk = MLA_HEADS * QK_NOPE
    slot = pl.BlockSpec((tm, MLA_HEADS * HEAD_W), lambda i: (i, 0))
    shape = jax.ShapeDtypeStruct((ROWS, MLA_HEADS * HEAD_W), BF16)
    return pl.pallas_call(
        _kvproj_kernel,
        grid=(ROWS // tm,),
        in_specs=[
            pl.BlockSpec((tm, KV_LORA), lambda i: (i, T_CKV // KV_LORA)),
            pl.BlockSpec((tm, LANES), lambda i: (i, T_KR // LANES)),
            pl.BlockSpec((tm, LANES), lambda i: (_rope_block_of_tile(i, tm), 0)),
            _gain_spec(layer, KV_LORA),
            pl.BlockSpec((None, KV_LORA, nk), lambda i: (layer, 0, 0)),
            pl.BlockSpec((None, KV_LORA, MLA_W), lambda i: (layer, 0, nk // MLA_W)),
        ],
        out_specs=[slot, slot],
        out_shape=[shape, shape],
        compiler_params=_cparams(("parallel",)),
        name="kv_proj",
    )(u, u, rope, g, w, w)


_NT = (((1,), (1,)), ((), ()))
ATTN_TQ = 2048
ATTN_CHUNK = 256


def _softmax_pv(s_blocks, v_refs):
    m = functools.reduce(jnp.maximum, [jnp.max(s, axis=-1, keepdims=True) for s in s_blocks])
    o = None
    for s, v_ref in zip(s_blocks, v_refs):
        pv = jnp.dot(jnp.exp2(s - m).astype(BF16), v_ref[...], preferred_element_type=F32)
        o = pv if o is None else o + pv
    return o[:, :V_DIM] / o[:, V_DIM:]


def _attn_latent_kernel(q_ref, kx_ref, kc_ref, vx_ref, vc_ref, o_ref):
    for c in range(ATTN_TQ // ATTN_CHUNK):
        rows = slice(c * ATTN_CHUNK, (c + 1) * ATTN_CHUNK)
        q = q_ref[rows, :]
        s1 = lax.dot_general(q, kx_ref[...], _NT, preferred_element_type=F32)
        s2 = lax.dot_general(q, kc_ref[...], _NT, preferred_element_type=F32)
        o_ref[rows, :] = _softmax_pv([s1, s2], [vx_ref, vc_ref]).astype(o_ref.dtype)


def _attn_latent(q, k, v):
    tq = ATTN_TQ
    nq = SEQ // tq
    cblk = ROWS_X // CTX_LEN
    return pl.pallas_call(
        _attn_latent_kernel,
        grid=(BATCH, MLA_HEADS, nq),
        in_specs=[
            pl.BlockSpec((tq, HEAD_W), lambda b, h, i: (b * nq + i, h)),
            pl.BlockSpec((SEQ, HEAD_W), lambda b, h, i: (b, h)),
            pl.BlockSpec((CTX_LEN, HEAD_W), lambda b, h, i: (cblk + b, h)),
            pl.BlockSpec((SEQ, HEAD_W), lambda b, h, i: (b, h)),
            pl.BlockSpec((CTX_LEN, HEAD_W), lambda b, h, i: (cblk + b, h)),
        ],
        out_specs=pl.BlockSpec((tq, V_DIM), lambda b, h, i: (b * nq + i, h)),
        out_shape=jax.ShapeDtypeStruct((ROWS_X, MLA_W), BF16),
        compiler_params=_cparams(("parallel", "parallel", "arbitrary")),
        name="attn_latent",
    )(q, k, k, v, v)


def _attn_ctx_kernel(q_ref, k_ref, v_ref, o_ref):
    s = lax.dot_general(q_ref[...], k_ref[...], _NT, preferred_element_type=F32)
    o_ref[...] = _softmax_pv([s], [v_ref]).astype(o_ref.dtype)


def _attn_ctx(q, k, v):
    cblk = ROWS_X // CTX_LEN
    slot = pl.BlockSpec((CTX_LEN, HEAD_W), lambda b, h: (cblk + b, h))
    return pl.pallas_call(
        _attn_ctx_kernel,
        grid=(BATCH, MLA_HEADS),
        in_specs=[slot, slot, slot],
        out_specs=pl.BlockSpec((CTX_LEN, V_DIM), lambda b, h: (b, h)),
        out_shape=jax.ShapeDtypeStruct((ROWS_C, MLA_W), BF16),
        compiler_params=_cparams(("parallel", "parallel")),
        name="attn_ctx",
    )(q, k, v)


MERGE_TM = 512
MERGE_TN = 512


def _merge_kernel(*refs, with_ctx):
    n_act = 6 if with_ctx else 3
    acts = refs[:n_act]
    hg_ref, wf_ref, wc_ref, wm_ref, gf_ref, gc_ref, gm_ref, bf_ref, bc_ref, bm_ref, y_ref = refs[n_act:]
    if with_ctx:
        is_latent = pl.program_id(0) < ROWS_X // MERGE_TM
        f, c, m = (jnp.where(is_latent, acts[k][...], acts[k + 3][...]) for k in range(3))
    else:
        f, c, m = (r[...] for r in acts)
    hg = hg_ref[...]

    def gated(x, w_ref, wg_ref, bg_ref):
        gate = jax.nn.sigmoid(jnp.dot(hg, wg_ref[...], preferred_element_type=F32) + bg_ref[...])
        return gate * jnp.dot(x, w_ref[...], preferred_element_type=F32)

    y = gated(f, wf_ref, gf_ref, bf_ref) + gated(c, wc_ref, gc_ref, bc_ref) + gated(m, wm_ref, gm_ref, bm_ref)
    y_ref[...] = y.astype(y_ref.dtype)


def _merge(latent, ctx, u, w_pf, w_pc, w_pm, w_gb, b_g, layer):
    tm, tn = MERGE_TM, MERGE_TN
    nb = D_MODEL // tn
    nxt = ROWS_X // tm
    with_ctx = ctx is not None
    n_rows = ROWS if with_ctx else ROWS_X
    widths = (FOURIER_W, CONV_W, MLA_W)

    act_specs = [pl.BlockSpec((tm, w), lambda i, j: (jnp.minimum(i, nxt - 1), 0)) for w in widths]
    if with_ctx:
        act_specs += [pl.BlockSpec((tm, w), lambda i, j: (jnp.maximum(i - nxt, 0), 0)) for w in widths]

    def proj_w(w):
        return pl.BlockSpec((None, w, tn), lambda i, j: (layer, 0, j))

    def gate_w(k):
        return pl.BlockSpec((None, GATE_RANK, tn), lambda i, j: (layer, 0, k * nb + j))

    def gate_b(k):
        return pl.BlockSpec((None, 1, tn), lambda i, j: (layer, 0, k * nb + j))

    return pl.pallas_call(
        functools.partial(_merge_kernel, with_ctx=with_ctx),
        grid=(n_rows // tm, nb),
        in_specs=act_specs + [
            pl.BlockSpec((tm, GATE_RANK), lambda i, j: (i, T_HG // GATE_RANK)),
            proj_w(FOURIER_W), proj_w(CONV_W), proj_w(MLA_W),
            gate_w(0), gate_w(1), gate_w(2),
            gate_b(0), gate_b(1), gate_b(2),
        ],
        out_specs=pl.BlockSpec((tm, tn), lambda i, j: (i, j)),
        out_shape=jax.ShapeDtypeStruct((n_rows, D_MODEL), BF16),
        compiler_params=_cparams(("parallel", "parallel")),
        name="merge",
    )(*latent, *(ctx or ()), u, w_pf, w_pc, w_pm, w_gb, w_gb, w_gb, b_g, b_g, b_g)


_ROPE_SWAP = tuple(
    ax * 2 * ROPE_FREQS + (1 - half) * ROPE_FREQS + f
    for ax in range(2) for half in range(2) for f in range(ROPE_FREQS))


def _rope_table(tm):
    n_rows = SEQ // GRID_W
    rows = jnp.repeat(jnp.arange(n_rows, dtype=F32), GRID_W)
    cols = jnp.tile(jnp.arange(GRID_W, dtype=F32), n_rows)
    pos = jnp.stack([rows, cols], axis=-1)
    inv_freq = jnp.power(ROPE_THETA, -jnp.arange(ROPE_FREQS, dtype=F32) / ROPE_FREQS)
    ang = pos[:, :, None] * inv_freq
    cos, sin = jnp.cos(ang), jnp.sin(ang)
    cos64 = jnp.stack([cos, cos], axis=2).reshape(SEQ, QK_ROPE)
    sin64 = jnp.stack([-sin, sin], axis=2).reshape(SEQ, QK_ROPE)
    latent = jnp.concatenate([cos64, sin64], axis=-1)
    ident = jnp.concatenate([jnp.ones((tm, QK_ROPE), F32), jnp.zeros((tm, QK_ROPE), F32)], axis=-1)
    return jnp.concatenate([latent, ident], axis=0)


def _prep_weights(w_in, w_gate_a, w_uq, w_ukv):
    swap = jnp.array(_ROPE_SWAP, dtype=jnp.int32)
    kr = w_in[:, :, OFF_KV + KV_LORA:OFF_KV + KV_LORA + QK_ROPE]
    w_tail = jnp.concatenate([
        w_gate_a.astype(BF16),
        w_in[:, :, OFF_KV:OFF_KV + KV_LORA].astype(BF16),
        kr.astype(BF16),
        kr[:, :, swap].astype(BF16),
        jnp.zeros((DEPTH, D_MODEL, T_W - T_KR - 2 * QK_ROPE), BF16),
    ], axis=2)
    wq = w_uq.reshape(DEPTH, Q_LORA, MLA_HEADS, QK_NOPE + QK_ROPE)
    qr = wq[..., QK_NOPE:]
    w_q = jnp.concatenate([wq[..., :QK_NOPE], qr, qr[..., swap]], axis=-1)
    w_q = w_q.reshape(DEPTH, Q_LORA, MLA_HEADS * HEAD_W).astype(BF16)
    wkv = w_ukv.reshape(DEPTH, KV_LORA, MLA_HEADS, QK_NOPE + V_DIM)
    w_kv = jnp.concatenate([
        wkv[..., :QK_NOPE].reshape(DEPTH, KV_LORA, MLA_HEADS * QK_NOPE),
        wkv[..., QK_NOPE:].reshape(DEPTH, KV_LORA, MLA_W),
    ], axis=2).astype(BF16)
    return w_tail, w_q, w_kv


def kernel(x, c, ctx, c_ctx, g_mix_pre, g_mix_post, g_mlp_pre, g_mlp_post, w_mod_a, w_mod_b, b_mod, w_in, conv_w, conv_b, conv_ln_g, conv_ln_b, q_norm_g, w_uq, kv_norm_g, w_ukv, w_pf, w_pc, w_pm, w_gate_a, w_gate_b, b_gate, w_out, w_ff1, w_ff2):
    rope = _rope_table(PROJ_TM)

    cc, sc = _dft_tables(FOURIER_GROUP_W)
    wc = (cc * (1.0 / 16.0)).astype(BF16)
    ws = (sc * (1.0 / 16.0)).astype(BF16)
    clx, slx = _dft_tables(SEQ)
    clx, slx = clx.astype(BF16), (-slx).astype(BF16)
    clc, slc = _dft_tables(CTX_LEN)
    clc, slc = clc.astype(BF16), (-slc).astype(BF16)

    def vec(a):
        return a[:, None, :]

    g_mix_pre, g_mix_post, g_mlp_pre, g_mlp_post = map(vec, (g_mix_pre, g_mix_post, g_mlp_pre, g_mlp_post))
    conv_b, conv_ln_g, conv_ln_b, q_norm_g, kv_norm_g, b_gate = map(
        vec, (conv_b, conv_ln_g, conv_ln_b, q_norm_g, kv_norm_g, b_gate))
    w_in = w_in.astype(BF16)
    w_tail, w_q, w_kv = _prep_weights(w_in, w_gate_a, w_uq, w_ukv)
    w_pf, w_pc, w_pm, w_gate_b = (w.astype(BF16) for w in (w_pf, w_pc, w_pm, w_gate_b))
    w_up = w_ff1[0].astype(BF16)
    small_tiles, big_tiles = 32, 128

    cond = jnp.concatenate(
        [c, c_ctx[None, :], jnp.zeros((N_GROUPS - BATCH - 1, D_MODEL), F32)], axis=0)
    mod = _adaln(cond, w_mod_a.astype(BF16), w_mod_b.astype(BF16), vec(b_mod))
    mod = mod.reshape(DEPTH, N_GROUPS, N_MOD, D_MODEL).transpose(0, 2, 1, 3)[:, :, :, None, :]

    t = jnp.concatenate([x.reshape(ROWS_X, D_MODEL), ctx.reshape(ROWS_C, D_MODEL)], axis=0)
    h = _modulate(t, g_mix_pre, mod, 0, 0, 1)

    for i in range(DEPTH):
        last = i == DEPTH - 1
        n_rows = ROWS_X if last else ROWS

        big = dict(tm=1024, tn=1024, tk=D_MODEL)
        u, w_o = _matmul(h, w_in, i, n_rows=n_rows, n_cols=OFF_KV, name="in_proj", **big,
                         cast=(w_out, i, D_MODEL, small_tiles))
        u_tail = _matmul(h, w_tail, i, n_rows=ROWS, tm=1024, tn=T_W, tk=D_MODEL, name="in_proj_tail")
        conv_args = (conv_w, conv_b, conv_ln_g, conv_ln_b, i)
        ab = _channel_dft(u, wc, ws, n_rows)
        q = _q_proj(u, q_norm_g, rope, w_q, i, n_rows)
        k, v = _kv_proj(u_tail, kv_norm_g, rope, w_kv, i)
        latent = (_seq_dft(ab, clx, slx, seq=SEQ, row0=0),
                  _conformer_conv(u, *conv_args, seq=SEQ, row0=0),
                  _attn_latent(q, k, v))
        context = None if last else (
            _seq_dft(ab, clc, slc, seq=CTX_LEN, row0=ROWS_X),
            _conformer_conv(u, *conv_args, seq=CTX_LEN, row0=ROWS_X),
            _attn_ctx(q, k, v))
        y = _merge(latent, context, u_tail, w_pf, w_pc, w_pm, w_gate_b, b_gate, i)
        yo = _matmul(y, w_o, n_rows=n_rows, name="out_proj", **big)
        t, h = _residual_modulate(t, yo, g_mix_post, mod, i, 2, g_mlp_pre, i, 3, 4, n_rows)

        hid, w_down = _matmul(h, w_up, n_rows=n_rows, sq_relu=True, name="ff1", **big,
                              cast=(w_ff2, i, D_MODEL, big_tiles))
        if last:
            yo = _matmul(hid, w_down, n_rows=n_rows, name="ff2", **big)
            t = _residual(t, yo, g_mlp_post, mod, i, 5, n_rows)
        else:
            yo, w_up = _matmul(hid, w_down, n_rows=n_rows, name="ff2", **big,
                               cast=(w_ff1, i + 1, D_FF, big_tiles))
            t, h = _residual_modulate(t, yo, g_mlp_post, mod, i, 5, g_mix_pre, i + 1, 0, 1, n_rows)

    return t.reshape(BATCH, SEQ, D_MODEL)
```

```python
import functools
import math

import jax
import jax.numpy as jnp
from jax import lax
from jax.experimental import pallas as pl
from jax.experimental.pallas import tpu as pltpu

F32 = jnp.float32
BF16 = jnp.bfloat16

D_MODEL = 4096
BATCH = 4
SEQ = 2048
DEPTH = 4
GRID_W = 64
CTX_LEN = 256
FOURIER_GROUPS = 4
FOURIER_GROUP_W = D_MODEL // 16
FOURIER_W = FOURIER_GROUPS * FOURIER_GROUP_W
CONV_W = D_MODEL // 4
CONV_K = 31
MLA_HEADS = D_MODEL // 256
Q_LORA = D_MODEL // 4
KV_LORA = D_MODEL // 8
QK_NOPE = 128
QK_ROPE = 64
V_DIM = 128
MLA_W = MLA_HEADS * V_DIM
ROPE_THETA = 10000.0
ROPE_FREQS = QK_ROPE // 4
ATTN_SCALE = (QK_NOPE + QK_ROPE) ** -0.5
Q_SCALE = ATTN_SCALE * math.log2(math.e)
N_BRANCH = 3
GATE_RANK = 512
MOD_RANK = 256
N_MOD = 6
D_FF = 4 * D_MODEL
EPS = 1e-6
OFF_C = FOURIER_W
OFF_Q = OFF_C + 2 * CONV_W
OFF_KV = OFF_Q + Q_LORA

ROWS_X = BATCH * SEQ
ROWS_C = BATCH * CTX_LEN
ROWS = ROWS_X + ROWS_C
N_GROUPS = 8
CTX_GROUP = BATCH
HEAD_W = 2 * QK_NOPE
CONV_PAD = 16
LANES = 128

U_F = 0
U_A = U_F + FOURIER_W
U_G = U_A + CONV_W
U_Q = U_G + CONV_W
T_HG = 0
T_CKV = T_HG + GATE_RANK
T_KR = T_CKV + KV_LORA
T_W = 1280

VMEM_LIMIT = 56 * 1024 * 1024


def _cparams(sem, limit=VMEM_LIMIT):
    return pltpu.CompilerParams(dimension_semantics=sem, vmem_limit_bytes=limit)


def _group_of_tile(i, tm):
    return jnp.where(i < ROWS_X // tm, i // (SEQ // tm), CTX_GROUP)


def _rope_block_of_tile(i, tm):
    return jnp.where(i < ROWS_X // tm, i % (SEQ // tm), SEQ // tm)


def _rms(x, g):
    return x * lax.rsqrt(jnp.mean(x * x, axis=-1, keepdims=True) + EPS) * g


def _mod_spec(layer, k, tm):
    return pl.BlockSpec((None, None, None, 1, D_MODEL),
                        lambda i: (layer, k, _group_of_tile(i, tm), 0, 0))


def _gain_spec(layer, width=D_MODEL):
    return pl.BlockSpec((None, 1, width), lambda *_: (layer, 0, 0))


def _adaln_kernel(c_ref, wa_ref, wb_ref, b_ref, o_ref):
    c = c_ref[...]
    s = (c * jax.nn.sigmoid(c)).astype(BF16)
    t = jnp.dot(s, wa_ref[...], preferred_element_type=F32)
    o_ref[...] = jnp.dot(t.astype(BF16), wb_ref[...], preferred_element_type=F32) + b_ref[...]


def _adaln(cond, w_a, w_b, b):
    tn = D_MODEL
    n = N_MOD * D_MODEL
    return pl.pallas_call(
        _adaln_kernel,
        grid=(DEPTH, n // tn),
        in_specs=[
            pl.BlockSpec((N_GROUPS, D_MODEL), lambda l, j: (0, 0)),
            pl.BlockSpec((None, D_MODEL, MOD_RANK), lambda l, j: (l, 0, 0)),
            pl.BlockSpec((None, MOD_RANK, tn), lambda l, j: (l, 0, j)),
            pl.BlockSpec((None, 1, tn), lambda l, j: (l, 0, j)),
        ],
        out_specs=pl.BlockSpec((None, N_GROUPS, tn), lambda l, j: (l, 0, j)),
        out_shape=jax.ShapeDtypeStruct((DEPTH, N_GROUPS, n), F32),
        compiler_params=_cparams(("parallel", "parallel")),
        name="adaln",
    )(cond, w_a, w_b, b)


ROW_TM = 256


def _modulated(x, g_ref, sh_ref, sc_ref):
    return (_rms(x, g_ref[...]) * (1.0 + sc_ref[...]) + sh_ref[...]).astype(BF16)


def _modulate_kernel(x_ref, g_ref, sh_ref, sc_ref, o_ref):
    o_ref[...] = _modulated(x_ref[...], g_ref, sh_ref, sc_ref)


def _modulate(x, g, mod, layer, k_shift, k_scale):
    tm = ROW_TM
    row = pl.BlockSpec((tm, D_MODEL), lambda i: (i, 0))
    return pl.pallas_call(
        _modulate_kernel,
        grid=(ROWS // tm,),
        in_specs=[row, _gain_spec(layer), _mod_spec(layer, k_shift, tm), _mod_spec(layer, k_scale, tm)],
        out_specs=row,
        out_shape=jax.ShapeDtypeStruct((ROWS, D_MODEL), BF16),
        compiler_params=_cparams(("parallel",)),
        name="modulate",
    )(x, g, mod, mod)


def _residual_kernel(x_ref, y_ref, g_ref, gate_ref, o_ref):
    o_ref[...] = x_ref[...] + gate_ref[...] * _rms(y_ref[...].astype(F32), g_ref[...])


def _residual(x, y, g, mod, layer, k_gate, n_rows):
    tm = ROW_TM
    row = pl.BlockSpec((tm, D_MODEL), lambda i: (i, 0))
    return pl.pallas_call(
        _residual_kernel,
        grid=(n_rows // tm,),
        in_specs=[row, row, _gain_spec(layer), _mod_spec(layer, k_gate, tm)],
        out_specs=row,
        out_shape=jax.ShapeDtypeStruct((n_rows, D_MODEL), F32),
        compiler_params=_cparams(("parallel",)),
        name="residual",
    )(x, y, g, mod)


def _resmod_kernel(x_ref, y_ref, gpost_ref, gate_ref, gpre_ref, sh_ref, sc_ref, xo_ref, h_ref):
    xn = x_ref[...] + gate_ref[...] * _rms(y_ref[...].astype(F32), gpost_ref[...])
    xo_ref[...] = xn
    h_ref[...] = _modulated(xn, gpre_ref, sh_ref, sc_ref)


def _residual_modulate(x, y, g_post, mod, layer, k_gate, g_pre, layer_pre, k_shift, k_scale, n_rows):
    tm = ROW_TM
    row = pl.BlockSpec((tm, D_MODEL), lambda i: (i, 0))
    return pl.pallas_call(
        _resmod_kernel,
        grid=(n_rows // tm,),
        in_specs=[row, row, _gain_spec(layer), _mod_spec(layer, k_gate, tm),
                  _gain_spec(layer_pre), _mod_spec(layer_pre, k_shift, tm),
                  _mod_spec(layer_pre, k_scale, tm)],
        out_specs=[row, row],
        out_shape=[jax.ShapeDtypeStruct((n_rows, D_MODEL), F32),
                   jax.ShapeDtypeStruct((n_rows, D_MODEL), BF16)],
        compiler_params=_cparams(("parallel",)),
        name="residual_modulate",
    )(x, y, g_post, mod, g_pre, mod, mod)


def _mm_kernel(*refs, nk, sq_relu, with_cast):
    if with_cast:
        a_ref, w_ref, src_ref, o_ref, cast_ref, *scratch = refs
        cast_ref[...] = src_ref[...].astype(BF16)
    else:
        a_ref, w_ref, o_ref, *scratch = refs

    def finish(r):
        if sq_relu:
            r = jnp.square(jnp.maximum(r, 0.0))
        o_ref[...] = r.astype(o_ref.dtype)

    p = jnp.dot(a_ref[...], w_ref[...], preferred_element_type=F32)
    if nk == 1:
        finish(p)
        return
    acc_ref, = scratch
    k = pl.program_id(2)

    @pl.when(k == 0)
    def _():
        acc_ref[...] = p

    @pl.when(k > 0)
    def _():
        acc_ref[...] += p

    @pl.when(k == nk - 1)
    def _():
        finish(acc_ref[...])


def _matmul(a, w, layer=None, *, n_rows, tm, tn, tk, n_cols=None, out_dtype=BF16, sq_relu=False, name,
            cast=None):
    kdim = w.shape[-2]
    n = n_cols or w.shape[-1]
    nj, nk = n // tn, kdim // tk
    if w.ndim == 3:
        w_spec = pl.BlockSpec((None, tk, tn), lambda i, j, k: (layer, k, j))
    else:
        w_spec = pl.BlockSpec((tk, tn), lambda i, j, k: (k, j))
    in_specs = [pl.BlockSpec((tm, tk), lambda i, j, k: (i, k)), w_spec]
    out_specs = [pl.BlockSpec((tm, tn), lambda i, j, k: (i, j))]
    out_shape = [jax.ShapeDtypeStruct((n_rows, n), out_dtype)]
    operands = [a, w]
    if cast is not None:
        src, src_layer, n_cols, n_tiles = cast
        assert n_tiles <= (n_rows // tm) * nj * nk
        src_rows = src.shape[1]
        tile_rows = src_rows // n_tiles

        def tile(i, j, k):
            return jnp.minimum((i * nj + j) * nk + k, n_tiles - 1)

        in_specs.append(pl.BlockSpec((None, tile_rows, n_cols), lambda i, j, k: (src_layer, tile(i, j, k), 0)))
        out_specs.append(pl.BlockSpec((tile_rows, n_cols), lambda i, j, k: (tile(i, j, k), 0)))
        out_shape.append(jax.ShapeDtypeStruct((src_rows, n_cols), BF16))
        operands.append(src)
    res = pl.pallas_call(
        functools.partial(_mm_kernel, nk=nk, sq_relu=sq_relu, with_cast=cast is not None),
        grid=(n_rows // tm, nj, nk),
        in_specs=in_specs,
        out_specs=out_specs,
        out_shape=out_shape,
        scratch_shapes=[pltpu.VMEM((tm, tn), F32)] if nk > 1 else [],
        compiler_params=_cparams(("arbitrary", "arbitrary", "arbitrary")),
        name=name,
    )(*operands)
    return res if cast is not None else res[0]


FF_TM = 1024
FF_TN = 1024


def _ff1_kernel(x_ref, y_ref, gpost_ref, gate_ref, gpre_ref, sh_ref, sc_ref, w_ref, src_ref,
                xo_ref, o_ref, cast_ref, h_even_ref, h_odd_ref):
    i, j = pl.program_id(0), pl.program_id(1)

    def ride(h_ref):
        cast_ref[...] = src_ref[...].astype(BF16)
        xn = x_ref[...] + gate_ref[...] * _rms(y_ref[...].astype(F32), gpost_ref[...])
        xo_ref[...] = xn
        rows = x_ref.shape[0]
        row0 = pl.multiple_of(j * rows, rows)
        h_ref[pl.ds(row0, rows), :] = _modulated(xn, gpre_ref, sh_ref, sc_ref)

    def ride_and_multiply(h_next_ref, h_ref):
        ride(h_next_ref)
        r = jnp.dot(h_ref[...], w_ref[...], preferred_element_type=F32)
        o_ref[...] = jnp.square(jnp.maximum(r, 0.0)).astype(o_ref.dtype)

    @pl.when(i == 0)
    def _():
        ride(h_even_ref)

    @pl.when(jnp.logical_and(i > 0, i % 2 == 1))
    def _():
        ride_and_multiply(h_odd_ref, h_even_ref)

    @pl.when(jnp.logical_and(i > 0, i % 2 == 0))
    def _():
        ride_and_multiply(h_even_ref, h_odd_ref)


def _ff1_fused(x, y, g_post, mod, layer, g_pre, w, cast, n_rows):
    tm, tn = FF_TM, FF_TN
    n = w.shape[1]
    ni, nj = n_rows // tm, n // tn
    rows = tm // nj
    src, src_layer, n_cols, n_tiles = cast
    assert n_tiles <= (ni + 1) * nj and tm % nj == 0 and rows % 16 == 0
    tile_rows = src.shape[1] // n_tiles

    def tile_of(i):
        return jnp.minimum(i, ni - 1)

    def cast_tile(i, j):
        return jnp.minimum(i * nj + j, n_tiles - 1)

    def col_of(i, j):
        return jnp.where(i == 0, 0, j)

    def slice_of(i, j):
        return jnp.where(i < ni, i * nj + j, ni * nj - 1)

    slice_spec = pl.BlockSpec((rows, D_MODEL), lambda i, j: (slice_of(i, j), 0))

    def mod_spec(k):
        return pl.BlockSpec((None, None, None, 1, D_MODEL),
                            lambda i, j: (layer, k, _group_of_tile(tile_of(i), tm), 0, 0))

    return pl.pallas_call(
        _ff1_kernel,
        grid=(ni + 1, nj),
        in_specs=[
            slice_spec, slice_spec, _gain_spec(layer), mod_spec(2),
            _gain_spec(layer), mod_spec(3), mod_spec(4),
            pl.BlockSpec((D_MODEL, tn), lambda i, j: (0, col_of(i, j))),
            pl.BlockSpec((None, tile_rows, n_cols), lambda i, j: (src_layer, cast_tile(i, j), 0)),
        ],
        out_specs=[
            slice_spec,
            pl.BlockSpec((tm, tn), lambda i, j: (jnp.maximum(i - 1, 0), col_of(i, j))),
            pl.BlockSpec((tile_rows, n_cols), lambda i, j: (cast_tile(i, j), 0)),
        ],
        out_shape=[
            jax.ShapeDtypeStruct((n_rows, D_MODEL), F32),
            jax.ShapeDtypeStruct((n_rows, n), BF16),
            jax.ShapeDtypeStruct((src.shape[1], n_cols), BF16),
        ],
        scratch_shapes=[pltpu.VMEM((tm, D_MODEL), BF16), pltpu.VMEM((tm, D_MODEL), BF16)],
        compiler_params=_cparams(("arbitrary", "arbitrary")),
        name="ff1",
    )(x, y, g_post, mod, g_pre, mod, mod, w, src)


def _chdft_kernel(u_ref, wc_ref, ws_ref, ab_ref):
    gw = FOURIER_GROUP_W
    for g in range(FOURIER_GROUPS):
        ug = u_ref[:, g * gw:(g + 1) * gw]
        ab_ref[:, g * gw:(g + 1) * gw] = jnp.dot(
            ug, wc_ref[...], preferred_element_type=F32).astype(BF16)
        ab_ref[:, FOURIER_W + g * gw:FOURIER_W + (g + 1) * gw] = jnp.dot(
            ug, ws_ref[...], preferred_element_type=F32).astype(BF16)


def _channel_dft(u, wc, ws, n_rows):
    tm = 1024
    gw = FOURIER_GROUP_W
    return pl.pallas_call(
        _chdft_kernel,
        grid=(n_rows // tm,),
        in_specs=[
            pl.BlockSpec((tm, FOURIER_W), lambda i: (i, U_F // FOURIER_W)),
            pl.BlockSpec((gw, gw), lambda i: (0, 0)),
            pl.BlockSpec((gw, gw), lambda i: (0, 0)),
        ],
        out_specs=pl.BlockSpec((tm, 2 * FOURIER_W), lambda i: (i, 0)),
        out_shape=jax.ShapeDtypeStruct((n_rows, 2 * FOURIER_W), BF16),
        compiler_params=_cparams(("parallel",)),
        name="channel_dft",
    )(u, wc, ws)


def _seqdft_kernel(cl_ref, sl_ref, a_ref, b_ref, o_ref, *, scale):
    r = jnp.dot(cl_ref[...], a_ref[...], preferred_element_type=F32)
    r = r + jnp.dot(sl_ref[...], b_ref[...], preferred_element_type=F32)
    o_ref[...] = (r * scale).astype(o_ref.dtype)


def _seq_dft(ab, cl, sl, *, seq, row0):
    tm = min(seq, 1024)
    nm = seq // tm
    blk0 = row0 // seq
    return pl.pallas_call(
        functools.partial(_seqdft_kernel, scale=float(seq) ** -0.5),
        grid=(nm, BATCH),
        in_specs=[
            pl.BlockSpec((tm, seq), lambda m, b: (m, 0)),
            pl.BlockSpec((tm, seq), lambda m, b: (m, 0)),
            pl.BlockSpec((seq, FOURIER_W), lambda m, b: (blk0 + b, 0)),
            pl.BlockSpec((seq, FOURIER_W), lambda m, b: (blk0 + b, 1)),
        ],
        out_specs=pl.BlockSpec((tm, FOURIER_W), lambda m, b: (b * nm + m, 0)),
        out_shape=jax.ShapeDtypeStruct((BATCH * seq, FOURIER_W), BF16),
        compiler_params=_cparams(("parallel", "parallel")),
        name="seq_dft_%d" % seq,
    )(cl, sl, ab, ab)


def _dft_tables(n):
    idx = jnp.arange(n, dtype=jnp.int32)
    ang = ((idx[:, None] * idx[None, :]) % n).astype(F32) * (2.0 * jnp.pi / n)
    return jnp.cos(ang), jnp.sin(ang)


CONV_ROWS = 64


def _conv_kernel(a_ref, gt_ref, w_ref, b_ref, lg_ref, lb_ref, o_ref, vpad_ref, y_ref, *, seq):
    zeros = jnp.zeros((CONV_PAD, CONV_W), F32)
    vpad_ref[0:CONV_PAD, :] = zeros
    vpad_ref[CONV_PAD + seq:2 * CONV_PAD + seq, :] = zeros
    glu_rows = 256

    def glu(i, carry):
        r0 = pl.multiple_of(i * glu_rows, glu_rows)
        a = a_ref[pl.ds(r0, glu_rows), :].astype(F32)
        g = gt_ref[pl.ds(r0, glu_rows), :].astype(F32)
        vpad_ref[pl.ds(r0 + CONV_PAD, glu_rows), :] = a * jax.nn.sigmoid(g)
        return carry

    lax.fori_loop(0, seq // glu_rows, glu, 0)

    win_rows = CONV_ROWS + 2 * CONV_PAD

    def step(i, carry):
        r0 = pl.multiple_of(i * CONV_ROWS, CONV_ROWS)
        for c in range(CONV_W // LANES):
            cols = slice(c * LANES, (c + 1) * LANES)
            win = vpad_ref[pl.ds(r0, win_rows), cols]
            acc = jnp.zeros((CONV_ROWS, LANES), F32)
            for b in range(8):
                wb = win if b == 0 else pltpu.roll(win, win_rows - b, axis=0)
                for a in range(4):
                    j = 8 * a + b
                    if j == 0:
                        continue
                    acc = acc + wb[8 * a:8 * a + CONV_ROWS, :] * w_ref[j - 1:j, cols]
            y_ref[:, cols] = acc + b_ref[:, cols]
        y = y_ref[...]
        yc = y - jnp.mean(y, axis=-1, keepdims=True)
        n = yc * lax.rsqrt(jnp.mean(yc * yc, axis=-1, keepdims=True) + EPS)
        n = n * lg_ref[...] + lb_ref[...]
        o_ref[pl.ds(r0, CONV_ROWS), :] = (n * jax.nn.sigmoid(n)).astype(o_ref.dtype)
        return carry

    lax.fori_loop(0, seq // CONV_ROWS, step, 0)


def _conformer_conv(u, conv_w, conv_b, ln_g, ln_b, layer, *, seq, row0):
    blk0 = row0 // seq
    vec = _gain_spec(layer, CONV_W)
    return pl.pallas_call(
        functools.partial(_conv_kernel, seq=seq),
        grid=(BATCH,),
        in_specs=[
            pl.BlockSpec((seq, CONV_W), lambda b: (blk0 + b, U_A // CONV_W)),
            pl.BlockSpec((seq, CONV_W), lambda b: (blk0 + b, U_G // CONV_W)),
            pl.BlockSpec((None, CONV_K, CONV_W), lambda b: (layer, 0, 0)),
            vec, vec, vec,
        ],
        out_specs=pl.BlockSpec((seq, CONV_W), lambda b: (b, 0)),
        out_shape=jax.ShapeDtypeStruct((BATCH * seq, CONV_W), BF16),
        scratch_shapes=[
            pltpu.VMEM((seq + 2 * CONV_PAD, CONV_W), F32),
            pltpu.VMEM((CONV_ROWS, CONV_W), F32),
        ],
        compiler_params=_cparams(("parallel",)),
        name="conformer_conv_%d" % seq,
    )(u, u, conv_w, conv_b, ln_g, ln_b)


PROJ_TM = 512


def _rotate(blk, table):
    t = blk * table
    return t + pltpu.roll(t, QK_ROPE, axis=1)


def _qproj_kernel(cq_ref, g_ref, t_ref, w_ref, q_ref, cn_ref, *, heads_per_step):
    @pl.when(pl.program_id(1) == 0)
    def _():
        cn_ref[...] = _rms(cq_ref[...].astype(F32), g_ref[...]).astype(BF16)

    acc = jnp.dot(cn_ref[...], w_ref[...], preferred_element_type=F32) * Q_SCALE
    table = t_ref[...]
    for h in range(heads_per_step):
        lo = h * HEAD_W
        q_ref[:, lo:lo + QK_NOPE] = acc[:, lo:lo + QK_NOPE].astype(BF16)
        q_ref[:, lo + QK_NOPE:lo + HEAD_W] = _rotate(
            acc[:, lo + QK_NOPE:lo + HEAD_W], table).astype(BF16)


def _q_proj(u, g, rope, w, layer, n_rows):
    tm = PROJ_TM
    hps = 4
    tn = hps * HEAD_W
    return pl.pallas_call(
        functools.partial(_qproj_kernel, heads_per_step=hps),
        grid=(n_rows // tm, MLA_HEADS // hps),
        in_specs=[
            pl.BlockSpec((tm, Q_LORA), lambda i, j: (i, U_Q // Q_LORA)),
            _gain_spec(layer, Q_LORA),
            pl.BlockSpec((tm, LANES), lambda i, j: (_rope_block_of_tile(i, tm), 0)),
            pl.BlockSpec((None, Q_LORA, tn), lambda i, j: (layer, 0, j)),
        ],
        out_specs=pl.BlockSpec((tm, tn), lambda i, j: (i, j)),
        out_shape=jax.ShapeDtypeStruct((n_rows, MLA_HEADS * HEAD_W), BF16),
        scratch_shapes=[pltpu.VMEM((tm, Q_LORA), BF16)],
        compiler_params=_cparams(("parallel", "arbitrary")),
        name="q_proj",
    )(u, g, rope, w)


def _kvproj_kernel(ckv_ref, kr_ref, t_ref, g_ref, wk_ref, wv_ref, k_ref, v_ref):
    cn = _rms(ckv_ref[...].astype(F32), g_ref[...]).astype(BF16)
    kn = jnp.dot(cn, wk_ref[...], preferred_element_type=F32)
    r = _rotate(kr_ref[...].astype(F32), t_ref[...])
    lane = lax.broadcasted_iota(jnp.int32, r.shape, 1)
    kr = jnp.where(lane < QK_ROPE, r, 0.0).astype(BF16)
    for h in range(MLA_HEADS):
        k_ref[:, h * HEAD_W:h * HEAD_W + QK_NOPE] = kn[:, h * QK_NOPE:(h + 1) * QK_NOPE].astype(BF16)
        k_ref[:, h * HEAD_W + QK_NOPE:(h + 1) * HEAD_W] = kr
    v = jnp.dot(cn, wv_ref[...], preferred_element_type=F32)
    ones = jnp.ones((v.shape[0], V_DIM), BF16)
    for h in range(MLA_HEADS):
        v_ref[:, h * HEAD_W:h * HEAD_W + V_DIM] = v[:, h * V_DIM:(h + 1) * V_DIM].astype(BF16)
        v_ref[:, h * HEAD_W + V_DIM:(h + 1) * HEAD_W] = ones


def _kv_proj(u, g, rope, w, layer):
    tm = PROJ_TM
    nk = MLA_HEADS * QK_NOPE
    slot = pl.BlockSpec((tm, MLA_HEADS * HEAD_W), lambda i: (i, 0))
    shape = jax.ShapeDtypeStruct((ROWS, MLA_HEADS * HEAD_W), BF16)
    return pl.pallas_call(
        _kvproj_kernel,
        grid=(ROWS // tm,),
        in_specs=[
            pl.BlockSpec((tm, KV_LORA), lambda i: (i, T_CKV // KV_LORA)),
            pl.BlockSpec((tm, LANES), lambda i: (i, T_KR // LANES)),
            pl.BlockSpec((tm, LANES), lambda i: (_rope_block_of_tile(i, tm), 0)),
            _gain_spec(layer, KV_LORA),
            pl.BlockSpec((None, KV_LORA, nk), lambda i: (layer, 0, 0)),
            pl.BlockSpec((None, KV_LORA, MLA_W), lambda i: (layer, 0, nk // MLA_W)),
        ],
        out_specs=[slot, slot],
        out_shape=[shape, shape],
        compiler_params=_cparams(("parallel",)),
        name="kv_proj",
    )(u, u, rope, g, w, w)


_NT = (((1,), (1,)), ((), ()))
ATTN_TQ = 2048
ATTN_CHUNK = 256


def _softmax_pv(s_blocks, v_refs):
    m = functools.reduce(jnp.maximum, [jnp.max(s, axis=-1, keepdims=True) for s in s_blocks])
    o = None
    for s, v_ref in zip(s_blocks, v_refs):
        pv = jnp.dot(jnp.exp2(s - m).astype(BF16), v_ref[...], preferred_element_type=F32)
        o = pv if o is None else o + pv
    return o[:, :V_DIM] / o[:, V_DIM:]


def _attn_latent_kernel(q_ref, kx_ref, kc_ref, vx_ref, vc_ref, o_ref):
    for c in range(ATTN_TQ // ATTN_CHUNK):
        rows = slice(c * ATTN_CHUNK, (c + 1) * ATTN_CHUNK)
        q = q_ref[rows, :]
        s1 = lax.dot_general(q, kx_ref[...], _NT, preferred_element_type=F32)
        s2 = lax.dot_general(q, kc_ref[...], _NT, preferred_element_type=F32)
        o_ref[rows, :] = _softmax_pv([s1, s2], [vx_ref, vc_ref]).astype(o_ref.dtype)


def _attn_latent(q, k, v):
    tq = ATTN_TQ
    nq = SEQ // tq
    cblk = ROWS_X // CTX_LEN
    return pl.pallas_call(
        _attn_latent_kernel,
        grid=(BATCH, MLA_HEADS, nq),
        in_specs=[
            pl.BlockSpec((tq, HEAD_W), lambda b, h, i: (b * nq + i, h)),
            pl.BlockSpec((SEQ, HEAD_W), lambda b, h, i: (b, h)),
            pl.BlockSpec((CTX_LEN, HEAD_W), lambda b, h, i: (cblk + b, h)),
            pl.BlockSpec((SEQ, HEAD_W), lambda b, h, i: (b, h)),
            pl.BlockSpec((CTX_LEN, HEAD_W), lambda b, h, i: (cblk + b, h)),
        ],
        out_specs=pl.BlockSpec((tq, V_DIM), lambda b, h, i: (b * nq + i, h)),
        out_shape=jax.ShapeDtypeStruct((ROWS_X, MLA_W), BF16),
        compiler_params=_cparams(("parallel", "parallel", "arbitrary")),
        name="attn_latent",
    )(q, k, k, v, v)


def _attn_ctx_kernel(q_ref, k_ref, v_ref, o_ref):
    s = lax.dot_general(q_ref[...], k_ref[...], _NT, preferred_element_type=F32)
    o_ref[...] = _softmax_pv([s], [v_ref]).astype(o_ref.dtype)


def _attn_ctx(q, k, v):
    cblk = ROWS_X // CTX_LEN
    slot = pl.BlockSpec((CTX_LEN, HEAD_W), lambda b, h: (cblk + b, h))
    return pl.pallas_call(
        _attn_ctx_kernel,
        grid=(BATCH, MLA_HEADS),
        in_specs=[slot, slot, slot],
        out_specs=pl.BlockSpec((CTX_LEN, V_DIM), lambda b, h: (b, h)),
        out_shape=jax.ShapeDtypeStruct((ROWS_C, MLA_W), BF16),
        compiler_params=_cparams(("parallel", "parallel")),
        name="attn_ctx",
    )(q, k, v)


MERGE_TM = 512
MERGE_TN = 512


def _merge_kernel(*refs, with_ctx):
    n_act = 6 if with_ctx else 3
    acts = refs[:n_act]
    hg_ref, wf_ref, wc_ref, wm_ref, gf_ref, gc_ref, gm_ref, bf_ref, bc_ref, bm_ref, y_ref = refs[n_act:]
    if with_ctx:
        is_latent = pl.program_id(0) < ROWS_X // MERGE_TM
        f, c, m = (jnp.where(is_latent, acts[k][...], acts[k + 3][...]) for k in range(3))
    else:
        f, c, m = (r[...] for r in acts)
    hg = hg_ref[...]

    def gated(x, w_ref, wg_ref, bg_ref):
        gate = jax.nn.sigmoid(jnp.dot(hg, wg_ref[...], preferred_element_type=F32) + bg_ref[...])
        return gate * jnp.dot(x, w_ref[...], preferred_element_type=F32)

    y = gated(f, wf_ref, gf_ref, bf_ref) + gated(c, wc_ref, gc_ref, bc_ref) + gated(m, wm_ref, gm_ref, bm_ref)
    y_ref[...] = y.astype(y_ref.dtype)


def _merge(latent, ctx, u, w_pf, w_pc, w_pm, w_gb, b_g, layer):
    tm, tn = MERGE_TM, MERGE_TN
    nb = D_MODEL // tn
    nxt = ROWS_X // tm
    with_ctx = ctx is not None
    n_rows = ROWS if with_ctx else ROWS_X
    widths = (FOURIER_W, CONV_W, MLA_W)

    act_specs = [pl.BlockSpec((tm, w), lambda i, j: (jnp.minimum(i, nxt - 1), 0)) for w in widths]
    if with_ctx:
        act_specs += [pl.BlockSpec((tm, w), lambda i, j: (jnp.maximum(i - nxt, 0), 0)) for w in widths]

    def proj_w(w):
        return pl.BlockSpec((None, w, tn), lambda i, j: (layer, 0, j))

    def gate_w(k):
        return pl.BlockSpec((None, GATE_RANK, tn), lambda i, j: (layer, 0, k * nb + j))

    def gate_b(k):
        return pl.BlockSpec((None, 1, tn), lambda i, j: (layer, 0, k * nb + j))

    return pl.pallas_call(
        functools.partial(_merge_kernel, with_ctx=with_ctx),
        grid=(n_rows // tm, nb),
        in_specs=act_specs + [
            pl.BlockSpec((tm, GATE_RANK), lambda i, j: (i, T_HG // GATE_RANK)),
            proj_w(FOURIER_W), proj_w(CONV_W), proj_w(MLA_W),
            gate_w(0), gate_w(1), gate_w(2),
            gate_b(0), gate_b(1), gate_b(2),
        ],
        out_specs=pl.BlockSpec((tm, tn), lambda i, j: (i, j)),
        out_shape=jax.ShapeDtypeStruct((n_rows, D_MODEL), BF16),
        compiler_params=_cparams(("parallel", "parallel")),
        name="merge",
    )(*latent, *(ctx or ()), u, w_pf, w_pc, w_pm, w_gb, w_gb, w_gb, b_g, b_g, b_g)


_ROPE_SWAP = tuple(
    ax * 2 * ROPE_FREQS + (1 - half) * ROPE_FREQS + f
    for ax in range(2) for half in range(2) for f in range(ROPE_FREQS))


def _rope_table(tm):
    n_rows = SEQ // GRID_W
    rows = jnp.repeat(jnp.arange(n_rows, dtype=F32), GRID_W)
    cols = jnp.tile(jnp.arange(GRID_W, dtype=F32), n_rows)
    pos = jnp.stack([rows, cols], axis=-1)
    inv_freq = jnp.power(ROPE_THETA, -jnp.arange(ROPE_FREQS, dtype=F32) / ROPE_FREQS)
    ang = pos[:, :, None] * inv_freq
    cos, sin = jnp.cos(ang), jnp.sin(ang)
    cos64 = jnp.stack([cos, cos], axis=2).reshape(SEQ, QK_ROPE)
    sin64 = jnp.stack([-sin, sin], axis=2).reshape(SEQ, QK_ROPE)
    latent = jnp.concatenate([cos64, sin64], axis=-1)
    ident = jnp.concatenate([jnp.ones((tm, QK_ROPE), F32), jnp.zeros((tm, QK_ROPE), F32)], axis=-1)
    return jnp.concatenate([latent, ident], axis=0)


def _prep_weights(w_in, w_gate_a, w_uq, w_ukv):
    swap = jnp.array(_ROPE_SWAP, dtype=jnp.int32)
    kr = w_in[:, :, OFF_KV + KV_LORA:OFF_KV + KV_LORA + QK_ROPE]
    w_tail = jnp.concatenate([
        w_gate_a.astype(BF16),
        w_in[:, :, OFF_KV:OFF_KV + KV_LORA].astype(BF16),
        kr.astype(BF16),
        kr[:, :, swap].astype(BF16),
        jnp.zeros((DEPTH, D_MODEL, T_W - T_KR - 2 * QK_ROPE), BF16),
    ], axis=2)
    wq = w_uq.reshape(DEPTH, Q_LORA, MLA_HEADS, QK_NOPE + QK_ROPE)
    qr = wq[..., QK_NOPE:]
    w_q = jnp.concatenate([wq[..., :QK_NOPE], qr, qr[..., swap]], axis=-1)
    w_q = w_q.reshape(DEPTH, Q_LORA, MLA_HEADS * HEAD_W).astype(BF16)
    wkv = w_ukv.reshape(DEPTH, KV_LORA, MLA_HEADS, QK_NOPE + V_DIM)
    w_kv = jnp.concatenate([
        wkv[..., :QK_NOPE].reshape(DEPTH, KV_LORA, MLA_HEADS * QK_NOPE),
        wkv[..., QK_NOPE:].reshape(DEPTH, KV_LORA, MLA_W),
    ], axis=2).astype(BF16)
    return w_tail, w_q, w_kv


def kernel(x, c, ctx, c_ctx, g_mix_pre, g_mix_post, g_mlp_pre, g_mlp_post, w_mod_a, w_mod_b, b_mod, w_in, conv_w, conv_b, conv_ln_g, conv_ln_b, q_norm_g, w_uq, kv_norm_g, w_ukv, w_pf, w_pc, w_pm, w_gate_a, w_gate_b, b_gate, w_out, w_ff1, w_ff2):
    rope = _rope_table(PROJ_TM)

    cc, sc = _dft_tables(FOURIER_GROUP_W)
    wc = (cc * (1.0 / 16.0)).astype(BF16)
    ws = (sc * (1.0 / 16.0)).astype(BF16)
    clx, slx = _dft_tables(SEQ)
    clx, slx = clx.astype(BF16), (-slx).astype(BF16)
    clc, slc = _dft_tables(CTX_LEN)
    clc, slc = clc.astype(BF16), (-slc).astype(BF16)

    def vec(a):
        return a[:, None, :]

    g_mix_pre, g_mix_post, g_mlp_pre, g_mlp_post = map(vec, (g_mix_pre, g_mix_post, g_mlp_pre, g_mlp_post))
    conv_b, conv_ln_g, conv_ln_b, q_norm_g, kv_norm_g, b_gate = map(
        vec, (conv_b, conv_ln_g, conv_ln_b, q_norm_g, kv_norm_g, b_gate))
    w_in = w_in.astype(BF16)
    w_tail, w_q, w_kv = _prep_weights(w_in, w_gate_a, w_uq, w_ukv)
    w_pf, w_pc, w_pm, w_gate_b = (w.astype(BF16) for w in (w_pf, w_pc, w_pm, w_gate_b))
    w_up = w_ff1[0].astype(BF16)
    small_tiles, big_tiles = 32, 128

    cond = jnp.concatenate(
        [c, c_ctx[None, :], jnp.zeros((N_GROUPS - BATCH - 1, D_MODEL), F32)], axis=0)
    mod = _adaln(cond, w_mod_a.astype(BF16), w_mod_b.astype(BF16), vec(b_mod))
    mod = mod.reshape(DEPTH, N_GROUPS, N_MOD, D_MODEL).transpose(0, 2, 1, 3)[:, :, :, None, :]

    t = jnp.concatenate([x.reshape(ROWS_X, D_MODEL), ctx.reshape(ROWS_C, D_MODEL)], axis=0)
    h = _modulate(t, g_mix_pre, mod, 0, 0, 1)

    for i in range(DEPTH):
        last = i == DEPTH - 1
        n_rows = ROWS_X if last else ROWS

        big = dict(tm=1024, tn=1024, tk=D_MODEL)
        u, w_o = _matmul(h, w_in, i, n_rows=n_rows, n_cols=OFF_KV, name="in_proj", **big,
                         cast=(w_out, i, D_MODEL, small_tiles))
        u_tail = _matmul(h, w_tail, i, n_rows=ROWS, tm=1024, tn=T_W, tk=D_MODEL, name="in_proj_tail")
        conv_args = (conv_w, conv_b, conv_ln_g, conv_ln_b, i)
        ab = _channel_dft(u, wc, ws, n_rows)
        q = _q_proj(u, q_norm_g, rope, w_q, i, n_rows)
        k, v = _kv_proj(u_tail, kv_norm_g, rope, w_kv, i)
        latent = (_seq_dft(ab, clx, slx, seq=SEQ, row0=0),
                  _conformer_conv(u, *conv_args, seq=SEQ, row0=0),
                  _attn_latent(q, k, v))
        context = None if last else (
            _seq_dft(ab, clc, slc, seq=CTX_LEN, row0=ROWS_X),
            _conformer_conv(u, *conv_args, seq=CTX_LEN, row0=ROWS_X),
            _attn_ctx(q, k, v))
        y = _merge(latent, context, u_tail, w_pf, w_pc, w_pm, w_gate_b, b_gate, i)
        yo = _matmul(y, w_o, n_rows=n_rows, name="out_proj", **big)

        t, hid, w_down = _ff1_fused(t, yo, g_mix_post, mod, i, g_mlp_pre, w_up,
                                    (w_ff2, i, D_MODEL, big_tiles), n_rows)
        if last:
            yo = _matmul(hid, w_down, n_rows=n_rows, name="ff2", **big)
            t = _residual(t, yo, g_mlp_post, mod, i, 5, n_rows)
        else:
            yo, w_up = _matmul(hid, w_down, n_rows=n_rows, name="ff2", **big,
                               cast=(w_ff1, i + 1, D_FF, big_tiles))
            t, h = _residual_modulate(t, yo, g_mlp_post, mod, i, 5, g_mix_pre, i + 1, 0, 1, n_rows)

    return t.reshape(BATCH, SEQ, D_MODEL)
```

```python
import functools
import math

import jax
import jax.numpy as jnp
from jax import lax
from jax.experimental import pallas as pl
from jax.experimental.pallas import tpu as pltpu

F32 = jnp.float32
BF16 = jnp.bfloat16

D_MODEL = 4096
BATCH = 4
SEQ = 2048
DEPTH = 4
GRID_W = 64
CTX_LEN = 256
FOURIER_GROUPS = 4
FOURIER_GROUP_W = D_MODEL // 16
FOURIER_W = FOURIER_GROUPS * FOURIER_GROUP_W
CONV_W = D_MODEL // 4
CONV_K = 31
MLA_HEADS = D_MODEL // 256
Q_LORA = D_MODEL // 4
KV_LORA = D_MODEL // 8
QK_NOPE = 128
QK_ROPE = 64
V_DIM = 128
MLA_W = MLA_HEADS * V_DIM
ROPE_THETA = 10000.0
ROPE_FREQS = QK_ROPE // 4
ATTN_SCALE = (QK_NOPE + QK_ROPE) ** -0.5
Q_SCALE = ATTN_SCALE * math.log2(math.e)
N_BRANCH = 3
GATE_RANK = 512
MOD_RANK = 256
N_MOD = 6
D_FF = 4 * D_MODEL
EPS = 1e-6
OFF_C = FOURIER_W
OFF_Q = OFF_C + 2 * CONV_W
OFF_KV = OFF_Q + Q_LORA

ROWS_X = BATCH * SEQ
ROWS_C = BATCH * CTX_LEN
ROWS = ROWS_X + ROWS_C
N_GROUPS = 8
CTX_GROUP = BATCH
HEAD_W = 2 * QK_NOPE
CONV_PAD = 16
LANES = 128

U_F = 0
U_A = U_F + FOURIER_W
U_G = U_A + CONV_W
U_Q = U_G + CONV_W
T_HG = 0
T_CKV = T_HG + GATE_RANK
T_KR = T_CKV + KV_LORA
T_W = 1280

VMEM_LIMIT = 56 * 1024 * 1024


def _cparams(sem, limit=VMEM_LIMIT):
    return pltpu.CompilerParams(dimension_semantics=sem, vmem_limit_bytes=limit)


def _group_of_tile(i, tm):
    return jnp.where(i < ROWS_X // tm, i // (SEQ // tm), CTX_GROUP)


def _rope_block_of_tile(i, tm):
    return jnp.where(i < ROWS_X // tm, i % (SEQ // tm), SEQ // tm)


def _rms(x, g):
    return x * lax.rsqrt(jnp.mean(x * x, axis=-1, keepdims=True) + EPS) * g


def _mod_spec(layer, k, tm):
    return pl.BlockSpec((None, None, None, 1, D_MODEL),
                        lambda i: (layer, k, _group_of_tile(i, tm), 0, 0))


def _gain_spec(layer, width=D_MODEL):
    return pl.BlockSpec((None, 1, width), lambda *_: (layer, 0, 0))


def _adaln_kernel(c_ref, wa_ref, wb_ref, b_ref, o_ref):
    c = c_ref[...]
    s = (c * jax.nn.sigmoid(c)).astype(BF16)
    t = jnp.dot(s, wa_ref[...], preferred_element_type=F32)
    o_ref[...] = jnp.dot(t.astype(BF16), wb_ref[...], preferred_element_type=F32) + b_ref[...]


def _adaln(cond, w_a, w_b, b):
    tn = D_MODEL
    n = N_MOD * D_MODEL
    return pl.pallas_call(
        _adaln_kernel,
        grid=(DEPTH, n // tn),
        in_specs=[
            pl.BlockSpec((N_GROUPS, D_MODEL), lambda l, j: (0, 0)),
            pl.BlockSpec((None, D_MODEL, MOD_RANK), lambda l, j: (l, 0, 0)),
            pl.BlockSpec((None, MOD_RANK, tn), lambda l, j: (l, 0, j)),
            pl.BlockSpec((None, 1, tn), lambda l, j: (l, 0, j)),
        ],
        out_specs=pl.BlockSpec((None, N_GROUPS, tn), lambda l, j: (l, 0, j)),
        out_shape=jax.ShapeDtypeStruct((DEPTH, N_GROUPS, n), F32),
        compiler_params=_cparams(("parallel", "parallel")),
        name="adaln",
    )(cond, w_a, w_b, b)


ROW_TM = 256


def _modulated(x, g_ref, sh_ref, sc_ref):
    return (_rms(x, g_ref[...]) * (1.0 + sc_ref[...]) + sh_ref[...]).astype(BF16)


def _modulate_kernel(x_ref, g_ref, sh_ref, sc_ref, o_ref):
    o_ref[...] = _modulated(x_ref[...], g_ref, sh_ref, sc_ref)


def _modulate(x, g, mod, layer, k_shift, k_scale):
    tm = ROW_TM
    row = pl.BlockSpec((tm, D_MODEL), lambda i: (i, 0))
    return pl.pallas_call(
        _modulate_kernel,
        grid=(ROWS // tm,),
        in_specs=[row, _gain_spec(layer), _mod_spec(layer, k_shift, tm), _mod_spec(layer, k_scale, tm)],
        out_specs=row,
        out_shape=jax.ShapeDtypeStruct((ROWS, D_MODEL), BF16),
        compiler_params=_cparams(("parallel",)),
        name="modulate",
    )(x, g, mod, mod)


def _residual_kernel(x_ref, y_ref, g_ref, gate_ref, o_ref):
    o_ref[...] = x_ref[...] + gate_ref[...] * _rms(y_ref[...].astype(F32), g_ref[...])


def _residual(x, y, g, mod, layer, k_gate, n_rows):
    tm = ROW_TM
    row = pl.BlockSpec((tm, D_MODEL), lambda i: (i, 0))
    return pl.pallas_call(
        _residual_kernel,
        grid=(n_rows // tm,),
        in_specs=[row, row, _gain_spec(layer), _mod_spec(layer, k_gate, tm)],
        out_specs=row,
        out_shape=jax.ShapeDtypeStruct((n_rows, D_MODEL), F32),
        compiler_params=_cparams(("parallel",)),
        name="residual",
    )(x, y, g, mod)


def _resmod_kernel(x_ref, y_ref, gpost_ref, gate_ref, gpre_ref, sh_ref, sc_ref, xo_ref, h_ref):
    xn = x_ref[...] + gate_ref[...] * _rms(y_ref[...].astype(F32), gpost_ref[...])
    xo_ref[...] = xn
    h_ref[...] = _modulated(xn, gpre_ref, sh_ref, sc_ref)


def _residual_modulate(x, y, g_post, mod, layer, k_gate, g_pre, layer_pre, k_shift, k_scale, n_rows):
    tm = ROW_TM
    row = pl.BlockSpec((tm, D_MODEL), lambda i: (i, 0))
    return pl.pallas_call(
        _resmod_kernel,
        grid=(n_rows // tm,),
        in_specs=[row, row, _gain_spec(layer), _mod_spec(layer, k_gate, tm),
                  _gain_spec(layer_pre), _mod_spec(layer_pre, k_shift, tm),
                  _mod_spec(layer_pre, k_scale, tm)],
        out_specs=[row, row],
        out_shape=[jax.ShapeDtypeStruct((n_rows, D_MODEL), F32),
                   jax.ShapeDtypeStruct((n_rows, D_MODEL), BF16)],
        compiler_params=_cparams(("parallel",)),
        name="residual_modulate",
    )(x, y, g_post, mod, g_pre, mod, mod)


def _mm_kernel(*refs, nk, sq_relu, with_cast):
    if with_cast:
        a_ref, w_ref, src_ref, o_ref, cast_ref, *scratch = refs
        cast_ref[...] = src_ref[...].astype(BF16)
    else:
        a_ref, w_ref, o_ref, *scratch = refs

    def finish(r):
        if sq_relu:
            r = jnp.square(jnp.maximum(r, 0.0))
        o_ref[...] = r.astype(o_ref.dtype)

    p = jnp.dot(a_ref[...], w_ref[...], preferred_element_type=F32)
    if nk == 1:
        finish(p)
        return
    acc_ref, = scratch
    k = pl.program_id(2)

    @pl.when(k == 0)
    def _():
        acc_ref[...] = p

    @pl.when(k > 0)
    def _():
        acc_ref[...] += p

    @pl.when(k == nk - 1)
    def _():
        finish(acc_ref[...])


def _matmul(a, w, layer=None, *, n_rows, tm, tn, tk, n_cols=None, out_dtype=BF16, sq_relu=False, name,
            cast=None):
    kdim = w.shape[-2]
    n = n_cols or w.shape[-1]
    nj, nk = n // tn, kdim // tk
    if w.ndim == 3:
        w_spec = pl.BlockSpec((None, tk, tn), lambda i, j, k: (layer, k, j))
    else:
        w_spec = pl.BlockSpec((tk, tn), lambda i, j, k: (k, j))
    in_specs = [pl.BlockSpec((tm, tk), lambda i, j, k: (i, k)), w_spec]
    out_specs = [pl.BlockSpec((tm, tn), lambda i, j, k: (i, j))]
    out_shape = [jax.ShapeDtypeStruct((n_rows, n), out_dtype)]
    operands = [a, w]
    if cast is not None:
        src, src_layer, n_cols, n_tiles = cast
        assert n_tiles <= (n_rows // tm) * nj * nk
        src_rows = src.shape[1]
        tile_rows = src_rows // n_tiles

        def tile(i, j, k):
            return jnp.minimum((i * nj + j) * nk + k, n_tiles - 1)

        in_specs.append(pl.BlockSpec((None, tile_rows, n_cols), lambda i, j, k: (src_layer, tile(i, j, k), 0)))
        out_specs.append(pl.BlockSpec((tile_rows, n_cols), lambda i, j, k: (tile(i, j, k), 0)))
        out_shape.append(jax.ShapeDtypeStruct((src_rows, n_cols), BF16))
        operands.append(src)
    res = pl.pallas_call(
        functools.partial(_mm_kernel, nk=nk, sq_relu=sq_relu, with_cast=cast is not None),
        grid=(n_rows // tm, nj, nk),
        in_specs=in_specs,
        out_specs=out_specs,
        out_shape=out_shape,
        scratch_shapes=[pltpu.VMEM((tm, tn), F32)] if nk > 1 else [],
        compiler_params=_cparams(("arbitrary", "arbitrary", "arbitrary")),
        name=name,
    )(*operands)
    return res if cast is not None else res[0]


NORM_TM = 1024


def _norm_mm_kernel(x_ref, y_ref, gpost_ref, gate_ref, gpre_ref, sh_ref, sc_ref, w_ref, src_ref,
                    xo_ref, o_ref, cast_ref, *rest, sq_relu, write_h):
    hout_ref = rest[0] if write_h else None
    h_even_ref, h_odd_ref = rest[-2:]
    i, j = pl.program_id(0), pl.program_id(1)

    def ride(h_ref):
        cast_ref[...] = src_ref[...].astype(BF16)
        xn = x_ref[...] + gate_ref[...] * _rms(y_ref[...].astype(F32), gpost_ref[...])
        xo_ref[...] = xn
        rows = x_ref.shape[0]
        row0 = pl.multiple_of(j * rows, rows)
        h = _modulated(xn, gpre_ref, sh_ref, sc_ref)
        h_ref[pl.ds(row0, rows), :] = h
        if write_h:
            hout_ref[...] = h

    def ride_and_multiply(h_next_ref, h_ref):
        ride(h_next_ref)
        r = jnp.dot(h_ref[...], w_ref[...], preferred_element_type=F32)
        if sq_relu:
            r = jnp.square(jnp.maximum(r, 0.0))
        o_ref[...] = r.astype(o_ref.dtype)

    @pl.when(i == 0)
    def _():
        ride(h_even_ref)

    @pl.when(jnp.logical_and(i > 0, i % 2 == 1))
    def _():
        ride_and_multiply(h_odd_ref, h_even_ref)

    @pl.when(jnp.logical_and(i > 0, i % 2 == 0))
    def _():
        ride_and_multiply(h_even_ref, h_odd_ref)


def _norm_matmul(x, y, g_post, mod, post, g_pre, pre, w, w_layer, cast, *, n_rows, n_cols, tn,
                 sq_relu, write_h, name):
    tm = NORM_TM
    ni, nj = n_rows // tm, n_cols // tn
    rows = tm // nj
    src, src_layer, src_cols, n_tiles = cast
    assert n_tiles <= (ni + 1) * nj and tm % nj == 0 and rows % 16 == 0
    tile_rows = src.shape[1] // n_tiles

    def tile_of(i):
        return jnp.minimum(i, ni - 1)

    def cast_tile(i, j):
        return jnp.minimum(i * nj + j, n_tiles - 1)

    def col_of(i, j):
        return jnp.where(i == 0, 0, j)

    def slice_of(i, j):
        return jnp.where(i < ni, i * nj + j, ni * nj - 1)

    slice_spec = pl.BlockSpec((rows, D_MODEL), lambda i, j: (slice_of(i, j), 0))

    def mod_spec(layer, k):
        return pl.BlockSpec((None, None, None, 1, D_MODEL),
                            lambda i, j: (layer, k, _group_of_tile(tile_of(i), tm), 0, 0))

    if w.ndim == 3:
        w_spec = pl.BlockSpec((None, D_MODEL, tn), lambda i, j: (w_layer, 0, col_of(i, j)))
    else:
        w_spec = pl.BlockSpec((D_MODEL, tn), lambda i, j: (0, col_of(i, j)))

    out_specs = [
        slice_spec,
        pl.BlockSpec((tm, tn), lambda i, j: (jnp.maximum(i - 1, 0), col_of(i, j))),
        pl.BlockSpec((tile_rows, src_cols), lambda i, j: (cast_tile(i, j), 0)),
    ]
    out_shape = [
        jax.ShapeDtypeStruct((n_rows, D_MODEL), F32),
        jax.ShapeDtypeStruct((n_rows, n_cols), BF16),
        jax.ShapeDtypeStruct((src.shape[1], src_cols), BF16),
    ]
    if write_h:
        out_specs.append(slice_spec)
        out_shape.append(jax.ShapeDtypeStruct((n_rows, D_MODEL), BF16))
    return pl.pallas_call(
        functools.partial(_norm_mm_kernel, sq_relu=sq_relu, write_h=write_h),
        grid=(ni + 1, nj),
        in_specs=[
            slice_spec, slice_spec, _gain_spec(post[0]), mod_spec(*post),
            _gain_spec(pre[0]), mod_spec(pre[0], pre[1]), mod_spec(pre[0], pre[2]),
            w_spec,
            pl.BlockSpec((None, tile_rows, src_cols), lambda i, j: (src_layer, cast_tile(i, j), 0)),
        ],
        out_specs=out_specs,
        out_shape=out_shape,
        scratch_shapes=[pltpu.VMEM((tm, D_MODEL), BF16), pltpu.VMEM((tm, D_MODEL), BF16)],
        compiler_params=_cparams(("arbitrary", "arbitrary")),
        name=name,
    )(x, y, g_post, mod, g_pre, mod, mod, w, src)


def _chdft_kernel(u_ref, wc_ref, ws_ref, ab_ref):
    gw = FOURIER_GROUP_W
    for g in range(FOURIER_GROUPS):
        ug = u_ref[:, g * gw:(g + 1) * gw]
        ab_ref[:, g * gw:(g + 1) * gw] = jnp.dot(
            ug, wc_ref[...], preferred_element_type=F32).astype(BF16)
        ab_ref[:, FOURIER_W + g * gw:FOURIER_W + (g + 1) * gw] = jnp.dot(
            ug, ws_ref[...], preferred_element_type=F32).astype(BF16)


def _channel_dft(u, wc, ws, n_rows):
    tm = 1024
    gw = FOURIER_GROUP_W
    return pl.pallas_call(
        _chdft_kernel,
        grid=(n_rows // tm,),
        in_specs=[
            pl.BlockSpec((tm, FOURIER_W), lambda i: (i, U_F // FOURIER_W)),
            pl.BlockSpec((gw, gw), lambda i: (0, 0)),
            pl.BlockSpec((gw, gw), lambda i: (0, 0)),
        ],
        out_specs=pl.BlockSpec((tm, 2 * FOURIER_W), lambda i: (i, 0)),
        out_shape=jax.ShapeDtypeStruct((n_rows, 2 * FOURIER_W), BF16),
        compiler_params=_cparams(("parallel",)),
        name="channel_dft",
    )(u, wc, ws)


def _seqdft_kernel(cl_ref, sl_ref, a_ref, b_ref, o_ref, *, scale):
    r = jnp.dot(cl_ref[...], a_ref[...], preferred_element_type=F32)
    r = r + jnp.dot(sl_ref[...], b_ref[...], preferred_element_type=F32)
    o_ref[...] = (r * scale).astype(o_ref.dtype)


def _seq_dft(ab, cl, sl, *, seq, row0):
    tm = min(seq, 1024)
    nm = seq // tm
    blk0 = row0 // seq
    return pl.pallas_call(
        functools.partial(_seqdft_kernel, scale=float(seq) ** -0.5),
        grid=(nm, BATCH),
        in_specs=[
            pl.BlockSpec((tm, seq), lambda m, b: (m, 0)),
            pl.BlockSpec((tm, seq), lambda m, b: (m, 0)),
            pl.BlockSpec((seq, FOURIER_W), lambda m, b: (blk0 + b, 0)),
            pl.BlockSpec((seq, FOURIER_W), lambda m, b: (blk0 + b, 1)),
        ],
        out_specs=pl.BlockSpec((tm, FOURIER_W), lambda m, b: (b * nm + m, 0)),
        out_shape=jax.ShapeDtypeStruct((BATCH * seq, FOURIER_W), BF16),
        compiler_params=_cparams(("parallel", "parallel")),
        name="seq_dft_%d" % seq,
    )(cl, sl, ab, ab)


def _dft_tables(n):
    idx = jnp.arange(n, dtype=jnp.int32)
    ang = ((idx[:, None] * idx[None, :]) % n).astype(F32) * (2.0 * jnp.pi / n)
    return jnp.cos(ang), jnp.sin(ang)


CONV_ROWS = 64


def _conv_kernel(a_ref, gt_ref, w_ref, b_ref, lg_ref, lb_ref, o_ref, vpad_ref, y_ref, *, seq):
    zeros = jnp.zeros((CONV_PAD, CONV_W), F32)
    vpad_ref[0:CONV_PAD, :] = zeros
    vpad_ref[CONV_PAD + seq:2 * CONV_PAD + seq, :] = zeros
    glu_rows = 256

    def glu(i, carry):
        r0 = pl.multiple_of(i * glu_rows, glu_rows)
        a = a_ref[pl.ds(r0, glu_rows), :].astype(F32)
        g = gt_ref[pl.ds(r0, glu_rows), :].astype(F32)
        vpad_ref[pl.ds(r0 + CONV_PAD, glu_rows), :] = a * jax.nn.sigmoid(g)
        return carry

    lax.fori_loop(0, seq // glu_rows, glu, 0)

    win_rows = CONV_ROWS + 2 * CONV_PAD

    def step(i, carry):
        r0 = pl.multiple_of(i * CONV_ROWS, CONV_ROWS)
        for c in range(CONV_W // LANES):
            cols = slice(c * LANES, (c + 1) * LANES)
            win = vpad_ref[pl.ds(r0, win_rows), cols]
            acc = jnp.zeros((CONV_ROWS, LANES), F32)
            for b in range(8):
                wb = win if b == 0 else pltpu.roll(win, win_rows - b, axis=0)
                for a in range(4):
                    j = 8 * a + b
                    if j == 0:
                        continue
                    acc = acc + wb[8 * a:8 * a + CONV_ROWS, :] * w_ref[j - 1:j, cols]
            y_ref[:, cols] = acc + b_ref[:, cols]
        y = y_ref[...]
        yc = y - jnp.mean(y, axis=-1, keepdims=True)
        n = yc * lax.rsqrt(jnp.mean(yc * yc, axis=-1, keepdims=True) + EPS)
        n = n * lg_ref[...] + lb_ref[...]
        o_ref[pl.ds(r0, CONV_ROWS), :] = (n * jax.nn.sigmoid(n)).astype(o_ref.dtype)
        return carry

    lax.fori_loop(0, seq // CONV_ROWS, step, 0)


def _conformer_conv(u, conv_w, conv_b, ln_g, ln_b, layer, *, seq, row0):
    blk0 = row0 // seq
    vec = _gain_spec(layer, CONV_W)
    return pl.pallas_call(
        functools.partial(_conv_kernel, seq=seq),
        grid=(BATCH,),
        in_specs=[
            pl.BlockSpec((seq, CONV_W), lambda b: (blk0 + b, U_A // CONV_W)),
            pl.BlockSpec((seq, CONV_W), lambda b: (blk0 + b, U_G // CONV_W)),
            pl.BlockSpec((None, CONV_K, CONV_W), lambda b: (layer, 0, 0)),
            vec, vec, vec,
        ],
        out_specs=pl.BlockSpec((seq, CONV_W), lambda b: (b, 0)),
        out_shape=jax.ShapeDtypeStruct((BATCH * seq, CONV_W), BF16),
        scratch_shapes=[
            pltpu.VMEM((seq + 2 * CONV_PAD, CONV_W), F32),
            pltpu.VMEM((CONV_ROWS, CONV_W), F32),
        ],
        compiler_params=_cparams(("parallel",)),
        name="conformer_conv_%d" % seq,
    )(u, u, conv_w, conv_b, ln_g, ln_b)


PROJ_TM = 512


def _rotate(blk, table):
    t = blk * table
    return t + pltpu.roll(t, QK_ROPE, axis=1)


def _qproj_kernel(cq_ref, g_ref, t_ref, w_ref, q_ref, cn_ref, *, heads_per_step):
    @pl.when(pl.program_id(1) == 0)
    def _():
        cn_ref[...] = _rms(cq_ref[...].astype(F32), g_ref[...]).astype(BF16)

    acc = jnp.dot(cn_ref[...], w_ref[...], preferred_element_type=F32) * Q_SCALE
    table = t_ref[...]
    for h in range(heads_per_step):
        lo = h * HEAD_W
        q_ref[:, lo:lo + QK_NOPE] = acc[:, lo:lo + QK_NOPE].astype(BF16)
        q_ref[:, lo + QK_NOPE:lo + HEAD_W] = _rotate(
            acc[:, lo + QK_NOPE:lo + HEAD_W], table).astype(BF16)


def _q_proj(u, g, rope, w, layer, n_rows):
    tm = PROJ_TM
    hps = 4
    tn = hps * HEAD_W
    return pl.pallas_call(
        functools.partial(_qproj_kernel, heads_per_step=hps),
        grid=(n_rows // tm, MLA_HEADS // hps),
        in_specs=[
            pl.BlockSpec((tm, Q_LORA), lambda i, j: (i, U_Q // Q_LORA)),
            _gain_spec(layer, Q_LORA),
            pl.BlockSpec((tm, LANES), lambda i, j: (_rope_block_of_tile(i, tm), 0)),
            pl.BlockSpec((None, Q_LORA, tn), lambda i, j: (layer, 0, j)),
        ],
        out_specs=pl.BlockSpec((tm, tn), lambda i, j: (i, j)),
        out_shape=jax.ShapeDtypeStruct((n_rows, MLA_HEADS * HEAD_W), BF16),
        scratch_shapes=[pltpu.VMEM((tm, Q_LORA), BF16)],
        compiler_params=_cparams(("parallel", "arbitrary")),
        name="q_proj",
    )(u, g, rope, w)


def _kvproj_kernel(ckv_ref, kr_ref, t_ref, g_ref, wk_ref, wv_ref, k_ref, v_ref):
    cn = _rms(ckv_ref[...].astype(F32), g_ref[...]).astype(BF16)
    kn = jnp.dot(cn, wk_ref[...], preferred_element_type=F32)
    r = _rotate(kr_ref[...].astype(F32), t_ref[...])
    lane = lax.broadcasted_iota(jnp.int32, r.shape, 1)
    kr = jnp.where(lane < QK_ROPE, r, 0.0).astype(BF16)
    for h in range(MLA_HEADS):
        k_ref[:, h * HEAD_W:h * HEAD_W + QK_NOPE] = kn[:, h * QK_NOPE:(h + 1) * QK_NOPE].astype(BF16)
        k_ref[:, h * HEAD_W + QK_NOPE:(h + 1) * HEAD_W] = kr
    v = jnp.dot(cn, wv_ref[...], preferred_element_type=F32)
    ones = jnp.ones((v.shape[0], V_DIM), BF16)
    for h in range(MLA_HEADS):
        v_ref[:, h * HEAD_W:h * HEAD_W + V_DIM] = v[:, h * V_DIM:(h + 1) * V_DIM].astype(BF16)
        v_ref[:, h * HEAD_W + V_DIM:(h + 1) * HEAD_W] = ones


def _kv_proj(u, g, rope, w, layer):
    tm = PROJ_TM
    nk = MLA_HEADS * QK_NOPE
    slot = pl.BlockSpec((tm, MLA_HEADS * HEAD_W), lambda i: (i, 0))
    shape = jax.ShapeDtypeStruct((ROWS, MLA_HEADS * HEAD_W), BF16)
    return pl.pallas_call(
        _kvproj_kernel,
        grid=(ROWS // tm,),
        in_specs=[
            pl.BlockSpec((tm, KV_LORA), lambda i: (i, T_CKV // KV_LORA)),
            pl.BlockSpec((tm, LANES), lambda i: (i, T_KR // LANES)),
            pl.BlockSpec((tm, LANES), lambda i: (_rope_block_of_tile(i, tm), 0)),
            _gain_spec(layer, KV_LORA),
            pl.BlockSpec((None, KV_LORA, nk), lambda i: (layer, 0, 0)),
            pl.BlockSpec((None, KV_LORA, MLA_W), lambda i: (layer, 0, nk // MLA_W)),
        ],
        out_specs=[slot, slot],
        out_shape=[shape, shape],
        compiler_params=_cparams(("parallel",)),
        name="kv_proj",
    )(u, u, rope, g, w, w)


_NT = (((1,), (1,)), ((), ()))
ATTN_TQ = 2048
ATTN_CHUNK = 256


def _softmax_pv(s_blocks, v_refs):
    m = functools.reduce(jnp.maximum, [jnp.max(s, axis=-1, keepdims=True) for s in s_blocks])
    o = None
    for s, v_ref in zip(s_blocks, v_refs):
        pv = jnp.dot(jnp.exp2(s - m).astype(BF16), v_ref[...], preferred_element_type=F32)
        o = pv if o is None else o + pv
    return o[:, :V_DIM] / o[:, V_DIM:]


def _attn_latent_kernel(q_ref, kx_ref, kc_ref, vx_ref, vc_ref, o_ref):
    for c in range(ATTN_TQ // ATTN_CHUNK):
        rows = slice(c * ATTN_CHUNK, (c + 1) * ATTN_CHUNK)
        q = q_ref[rows, :]
        s1 = lax.dot_general(q, kx_ref[...], _NT, preferred_element_type=F32)
        s2 = lax.dot_general(q, kc_ref[...], _NT, preferred_element_type=F32)
        o_ref[rows, :] = _softmax_pv([s1, s2], [vx_ref, vc_ref]).astype(o_ref.dtype)


def _attn_latent(q, k, v):
    tq = ATTN_TQ
    nq = SEQ // tq
    cblk = ROWS_X // CTX_LEN
    return pl.pallas_call(
        _attn_latent_kernel,
        grid=(BATCH, MLA_HEADS, nq),
        in_specs=[
            pl.BlockSpec((tq, HEAD_W), lambda b, h, i: (b * nq + i, h)),
            pl.BlockSpec((SEQ, HEAD_W), lambda b, h, i: (b, h)),
            pl.BlockSpec((CTX_LEN, HEAD_W), lambda b, h, i: (cblk + b, h)),
            pl.BlockSpec((SEQ, HEAD_W), lambda b, h, i: (b, h)),
            pl.BlockSpec((CTX_LEN, HEAD_W), lambda b, h, i: (cblk + b, h)),
        ],
        out_specs=pl.BlockSpec((tq, V_DIM), lambda b, h, i: (b * nq + i, h)),
        out_shape=jax.ShapeDtypeStruct((ROWS_X, MLA_W), BF16),
        compiler_params=_cparams(("parallel", "parallel", "arbitrary")),
        name="attn_latent",
    )(q, k, k, v, v)


def _attn_ctx_kernel(q_ref, k_ref, v_ref, o_ref):
    s = lax.dot_general(q_ref[...], k_ref[...], _NT, preferred_element_type=F32)
    o_ref[...] = _softmax_pv([s], [v_ref]).astype(o_ref.dtype)


def _attn_ctx(q, k, v):
    cblk = ROWS_X // CTX_LEN
    slot = pl.BlockSpec((CTX_LEN, HEAD_W), lambda b, h: (cblk + b, h))
    return pl.pallas_call(
        _attn_ctx_kernel,
        grid=(BATCH, MLA_HEADS),
        in_specs=[slot, slot, slot],
        out_specs=pl.BlockSpec((CTX_LEN, V_DIM), lambda b, h: (b, h)),
        out_shape=jax.ShapeDtypeStruct((ROWS_C, MLA_W), BF16),
        compiler_params=_cparams(("parallel", "parallel")),
        name="attn_ctx",
    )(q, k, v)


MERGE_TM = 512
MERGE_TN = 512


def _merge_kernel(*refs, with_ctx):
    n_act = 6 if with_ctx else 3
    acts = refs[:n_act]
    hg_ref, wf_ref, wc_ref, wm_ref, gf_ref, gc_ref, gm_ref, bf_ref, bc_ref, bm_ref, y_ref = refs[n_act:]
    if with_ctx:
        is_latent = pl.program_id(0) < ROWS_X // MERGE_TM
        f, c, m = (jnp.where(is_latent, acts[k][...], acts[k + 3][...]) for k in range(3))
    else:
        f, c, m = (r[...] for r in acts)
    hg = hg_ref[...]

    def gated(x, w_ref, wg_ref, bg_ref):
        gate = jax.nn.sigmoid(jnp.dot(hg, wg_ref[...], preferred_element_type=F32) + bg_ref[...])
        return gate * jnp.dot(x, w_ref[...], preferred_element_type=F32)

    y = gated(f, wf_ref, gf_ref, bf_ref) + gated(c, wc_ref, gc_ref, bc_ref) + gated(m, wm_ref, gm_ref, bm_ref)
    y_ref[...] = y.astype(y_ref.dtype)


def _merge(latent, ctx, u, w_pf, w_pc, w_pm, w_gb, b_g, layer):
    tm, tn = MERGE_TM, MERGE_TN
    nb = D_MODEL // tn
    nxt = ROWS_X // tm
    with_ctx = ctx is not None
    n_rows = ROWS if with_ctx else ROWS_X
    widths = (FOURIER_W, CONV_W, MLA_W)

    act_specs = [pl.BlockSpec((tm, w), lambda i, j: (jnp.minimum(i, nxt - 1), 0)) for w in widths]
    if with_ctx:
        act_specs += [pl.BlockSpec((tm, w), lambda i, j: (jnp.maximum(i - nxt, 0), 0)) for w in widths]

    def proj_w(w):
        return pl.BlockSpec((None, w, tn), lambda i, j: (layer, 0, j))

    def gate_w(k):
        return pl.BlockSpec((None, GATE_RANK, tn), lambda i, j: (layer, 0, k * nb + j))

    def gate_b(k):
        return pl.BlockSpec((None, 1, tn), lambda i, j: (layer, 0, k * nb + j))

    return pl.pallas_call(
        functools.partial(_merge_kernel, with_ctx=with_ctx),
        grid=(n_rows // tm, nb),
        in_specs=act_specs + [
            pl.BlockSpec((tm, GATE_RANK), lambda i, j: (i, T_HG // GATE_RANK)),
            proj_w(FOURIER_W), proj_w(CONV_W), proj_w(MLA_W),
            gate_w(0), gate_w(1), gate_w(2),
            gate_b(0), gate_b(1), gate_b(2),
        ],
        out_specs=pl.BlockSpec((tm, tn), lambda i, j: (i, j)),
        out_shape=jax.ShapeDtypeStruct((n_rows, D_MODEL), BF16),
        compiler_params=_cparams(("parallel", "parallel")),
        name="merge",
    )(*latent, *(ctx or ()), u, w_pf, w_pc, w_pm, w_gb, w_gb, w_gb, b_g, b_g, b_g)


_ROPE_SWAP = tuple(
    ax * 2 * ROPE_FREQS + (1 - half) * ROPE_FREQS + f
    for ax in range(2) for half in range(2) for f in range(ROPE_FREQS))


def _rope_table(tm):
    n_rows = SEQ // GRID_W
    rows = jnp.repeat(jnp.arange(n_rows, dtype=F32), GRID_W)
    cols = jnp.tile(jnp.arange(GRID_W, dtype=F32), n_rows)
    pos = jnp.stack([rows, cols], axis=-1)
    inv_freq = jnp.power(ROPE_THETA, -jnp.arange(ROPE_FREQS, dtype=F32) / ROPE_FREQS)
    ang = pos[:, :, None] * inv_freq
    cos, sin = jnp.cos(ang), jnp.sin(ang)
    cos64 = jnp.stack([cos, cos], axis=2).reshape(SEQ, QK_ROPE)
    sin64 = jnp.stack([-sin, sin], axis=2).reshape(SEQ, QK_ROPE)
    latent = jnp.concatenate([cos64, sin64], axis=-1)
    ident = jnp.concatenate([jnp.ones((tm, QK_ROPE), F32), jnp.zeros((tm, QK_ROPE), F32)], axis=-1)
    return jnp.concatenate([latent, ident], axis=0)


def _prep_weights(w_in, w_gate_a, w_uq, w_ukv):
    swap = jnp.array(_ROPE_SWAP, dtype=jnp.int32)
    kr = w_in[:, :, OFF_KV + KV_LORA:OFF_KV + KV_LORA + QK_ROPE]
    w_tail = jnp.concatenate([
        w_gate_a.astype(BF16),
        w_in[:, :, OFF_KV:OFF_KV + KV_LORA].astype(BF16),
        kr.astype(BF16),
        kr[:, :, swap].astype(BF16),
        jnp.zeros((DEPTH, D_MODEL, T_W - T_KR - 2 * QK_ROPE), BF16),
    ], axis=2)
    wq = w_uq.reshape(DEPTH, Q_LORA, MLA_HEADS, QK_NOPE + QK_ROPE)
    qr = wq[..., QK_NOPE:]
    w_q = jnp.concatenate([wq[..., :QK_NOPE], qr, qr[..., swap]], axis=-1)
    w_q = w_q.reshape(DEPTH, Q_LORA, MLA_HEADS * HEAD_W).astype(BF16)
    wkv = w_ukv.reshape(DEPTH, KV_LORA, MLA_HEADS, QK_NOPE + V_DIM)
    w_kv = jnp.concatenate([
        wkv[..., :QK_NOPE].reshape(DEPTH, KV_LORA, MLA_HEADS * QK_NOPE),
        wkv[..., QK_NOPE:].reshape(DEPTH, KV_LORA, MLA_W),
    ], axis=2).astype(BF16)
    return w_tail, w_q, w_kv


def kernel(x, c, ctx, c_ctx, g_mix_pre, g_mix_post, g_mlp_pre, g_mlp_post, w_mod_a, w_mod_b, b_mod, w_in, conv_w, conv_b, conv_ln_g, conv_ln_b, q_norm_g, w_uq, kv_norm_g, w_ukv, w_pf, w_pc, w_pm, w_gate_a, w_gate_b, b_gate, w_out, w_ff1, w_ff2):
    rope = _rope_table(PROJ_TM)

    cc, sc = _dft_tables(FOURIER_GROUP_W)
    wc = (cc * (1.0 / 16.0)).astype(BF16)
    ws = (sc * (1.0 / 16.0)).astype(BF16)
    clx, slx = _dft_tables(SEQ)
    clx, slx = clx.astype(BF16), (-slx).astype(BF16)
    clc, slc = _dft_tables(CTX_LEN)
    clc, slc = clc.astype(BF16), (-slc).astype(BF16)

    def vec(a):
        return a[:, None, :]

    g_mix_pre, g_mix_post, g_mlp_pre, g_mlp_post = map(vec, (g_mix_pre, g_mix_post, g_mlp_pre, g_mlp_post))
    conv_b, conv_ln_g, conv_ln_b, q_norm_g, kv_norm_g, b_gate = map(
        vec, (conv_b, conv_ln_g, conv_ln_b, q_norm_g, kv_norm_g, b_gate))
    w_in = w_in.astype(BF16)
    w_tail, w_q, w_kv = _prep_weights(w_in, w_gate_a, w_uq, w_ukv)
    w_pf, w_pc, w_pm, w_gate_b = (w.astype(BF16) for w in (w_pf, w_pc, w_pm, w_gate_b))
    w_up = w_ff1[0].astype(BF16)
    small_tiles, big_tiles = 32, 128

    cond = jnp.concatenate(
        [c, c_ctx[None, :], jnp.zeros((N_GROUPS - BATCH - 1, D_MODEL), F32)], axis=0)
    mod = _adaln(cond, w_mod_a.astype(BF16), w_mod_b.astype(BF16), vec(b_mod))
    mod = mod.reshape(DEPTH, N_GROUPS, N_MOD, D_MODEL).transpose(0, 2, 1, 3)[:, :, :, None, :]

    t = jnp.concatenate([x.reshape(ROWS_X, D_MODEL), ctx.reshape(ROWS_C, D_MODEL)], axis=0)
    h = _modulate(t, g_mix_pre, mod, 0, 0, 1)
    big = dict(tm=1024, tn=1024, tk=D_MODEL)
    u, w_o = _matmul(h, w_in, 0, n_rows=ROWS, n_cols=OFF_KV, name="in_proj_first", **big,
                     cast=(w_out, 0, D_MODEL, small_tiles))

    for i in range(DEPTH):
        last = i == DEPTH - 1
        n_rows = ROWS_X if last else ROWS

        u_tail = _matmul(h, w_tail, i, n_rows=ROWS, tm=1024, tn=T_W, tk=D_MODEL, name="in_proj_tail")
        conv_args = (conv_w, conv_b, conv_ln_g, conv_ln_b, i)
        ab = _channel_dft(u, wc, ws, n_rows)
        q = _q_proj(u, q_norm_g, rope, w_q, i, n_rows)
        k, v = _kv_proj(u_tail, kv_norm_g, rope, w_kv, i)
        latent = (_seq_dft(ab, clx, slx, seq=SEQ, row0=0),
                  _conformer_conv(u, *conv_args, seq=SEQ, row0=0),
                  _attn_latent(q, k, v))
        context = None if last else (
            _seq_dft(ab, clc, slc, seq=CTX_LEN, row0=ROWS_X),
            _conformer_conv(u, *conv_args, seq=CTX_LEN, row0=ROWS_X),
            _attn_ctx(q, k, v))
        y = _merge(latent, context, u_tail, w_pf, w_pc, w_pm, w_gate_b, b_gate, i)
        yo = _matmul(y, w_o, n_rows=n_rows, name="out_proj", **big)

        t, hid, w_down = _norm_matmul(
            t, yo, g_mix_post, mod, (i, 2), g_mlp_pre, (i, 3, 4), w_up, None,
            (w_ff2, i, D_MODEL, big_tiles), n_rows=n_rows, n_cols=D_FF, tn=1024,
            sq_relu=True, write_h=False, name="ff1")
        if last:
            yo = _matmul(hid, w_down, n_rows=n_rows, name="ff2", **big)
            t = _residual(t, yo, g_mlp_post, mod, i, 5, n_rows)
        else:
            yo, w_up = _matmul(hid, w_down, n_rows=n_rows, name="ff2", **big,
                               cast=(w_ff1, i + 1, D_FF, big_tiles))
            t, u, w_o, h = _norm_matmul(
                t, yo, g_mlp_post, mod, (i, 5), g_mix_pre, (i + 1, 0, 1), w_in, i + 1,
                (w_out, i + 1, D_MODEL, small_tiles), n_rows=ROWS, n_cols=OFF_KV, tn=512,
                sq_relu=False, write_h=True, name="in_proj")

    return t.reshape(BATCH, SEQ, D_MODEL)
```
